```python
import math
import jax, jax.numpy as jnp
from jax import lax
import numpy as np

D_MODEL = 1024
BATCH = 16
SEQ = 4096
DEPTH = 4

CHUNK = 64
Q_BLOCK = 128
HEAD_DIM = 64
N_BRANCH = 4
D_BRANCH = D_MODEL // 4
D_CONV = D_BRANCH
CONV_WIDTH = 31
D_POOL = D_BRANCH
POOL_WINDOWS = (2, 4, 8, 16)
POOL_GROUP = D_POOL // len(POOL_WINDOWS)
SB_HEADS = D_BRANCH // HEAD_DIM
RET_HEADS = D_BRANCH // HEAD_DIM
XA_HEADS = 4
D_XA = XA_HEADS * HEAD_DIM
N_MEM = 256
D_FF = 2816
N_EXPERTS = 8
TOP_K = 2
ROPE_THETA = 10000.0
EPS = 1e-6
IN_SPLITS = (2 * D_CONV, 2 * D_CONV + D_POOL, 2 * D_CONV + D_POOL + 3 * D_BRANCH)
D_IN = 2 * D_CONV + D_POOL + 3 * D_BRANCH + 4 * D_BRANCH

kernel_name = 'hybrid_conv_pool_stickbreak_retention_moe_trunk'


def _rms_norm(x, g):
    xf = x.astype(jnp.float32)
    y = xf * lax.rsqrt(jnp.mean(xf * xf, axis=-1, keepdims=True) + EPS)
    return (y * g.astype(jnp.float32)).astype(x.dtype)


def _layer_norm(x, g, b):
    xf = x.astype(jnp.float32)
    xc = xf - jnp.mean(xf, axis=-1, keepdims=True)
    y = xc * lax.rsqrt(jnp.mean(xc * xc, axis=-1, keepdims=True) + EPS)
    return (y * g.astype(jnp.float32) + b.astype(jnp.float32)).astype(x.dtype)


def _head_norm(x):
    xc = x - jnp.mean(x, axis=-1, keepdims=True)
    return xc * lax.rsqrt(jnp.mean(xc * xc, axis=-1, keepdims=True) + EPS)


def _rotary(x, positions):
    half = x.shape[-1] // 2
    inv_freq = ROPE_THETA ** (-jnp.arange(half, dtype=jnp.float32) / half)
    ang = positions.astype(jnp.float32)[:, :, None, None] * inv_freq
    cos, sin = jnp.cos(ang), jnp.sin(ang)
    x1, x2 = x[..., :half], x[..., half:]
    return jnp.concatenate([x1 * cos - x2 * sin, x1 * sin + x2 * cos], axis=-1)


def _swiglu(h, w_gu, w_down):
    g, u = jnp.split(h @ w_gu, 2, axis=-1)
    return (jax.nn.silu(g) * u) @ w_down


def _conformer_conv(u, conv_w, conv_b, ln_g, ln_b):
    a, g = jnp.split(u, 2, axis=-1)
    v = a * jax.nn.sigmoid(g)
    v = lax.conv_general_dilated(
        v, conv_w.astype(v.dtype)[:, None, :], window_strides=(1,),
        padding=[(CONV_WIDTH - 1, 0)], dimension_numbers=('NWC', 'WIO', 'NWC'),
        feature_group_count=D_CONV) + conv_b
    return jax.nn.silu(_layer_norm(v, ln_g, ln_b))


def _multiscale_pool(u, pool_w, pool_scale):
    B, S, _ = u.shape
    uf = u.astype(jnp.float32)
    cs = jnp.pad(jnp.cumsum(uf, axis=1), ((0, 0), (1, 0), (0, 0)))
    t = jnp.arange(S, dtype=jnp.float32)[None, :, None]
    diffs = []
    for gi, w in enumerate(POOL_WINDOWS):
        sl = slice(gi * POOL_GROUP, (gi + 1) * POOL_GROUP)
        c = cs[:, :, sl]
        lower = jnp.pad(c, ((0, 0), (w - 1, 0), (0, 0)))[:, :S]
        mean = (c[:, 1:] - lower) / jnp.minimum(t + 1.0, float(w))
        diffs.append(mean - uf[:, :, sl])
    d = jnp.stack(diffs, axis=2).astype(u.dtype)
    y = jnp.einsum('bsgc,gcd->bsgd', d, pool_w).reshape(B, S, D_POOL)
    return y * pool_scale


def _stick_breaking(q, k, v):
    B, S, H, Dh = q.shape
    scale = Dh ** -0.5
    outs = []
    for qs in range(0, S, Q_BLOCK):
        qe = qs + Q_BLOCK
        z = jnp.einsum('bqhd,bkhd->bhqk', q[:, qs:qe], k[:, :qe]).astype(jnp.float32) * scale
        t_idx = qs + jnp.arange(Q_BLOCK)[:, None]
        s_idx = jnp.arange(qe)[None, :]
        mask = s_idx < t_idx
        log_1m = jnp.where(mask, jax.nn.log_sigmoid(-z), 0.0)
        tail = lax.cumsum(log_1m, axis=log_1m.ndim - 1, reverse=True) - log_1m
        a = jnp.where(mask, jnp.exp(jax.nn.log_sigmoid(z) + tail), 0.0)
        outs.append(jnp.einsum('bhqk,bkhd->bqhd', a.astype(v.dtype), v[:, :qe]))
    return jnp.concatenate(outs, axis=1)


def _retention(q, k, v, g, positions, gammas):
    B, S, _ = q.shape
    nc = S // CHUNK
    hs = (B, S, RET_HEADS, HEAD_DIM)
    cs = (B, nc, CHUNK, RET_HEADS, HEAD_DIM)
    q = _rotary(q.astype(jnp.float32).reshape(hs), positions).reshape(cs)
    k = (_rotary(k.astype(jnp.float32).reshape(hs), positions) * HEAD_DIM ** -0.5).reshape(cs)
    v = v.astype(jnp.float32).reshape(cs)
    log_g = jnp.log(gammas)
    idx = jnp.arange(CHUNK, dtype=jnp.float32)
    rel = idx[:, None] - idx[None, :]
    decay = jnp.where(rel >= 0, jnp.exp(jnp.maximum(rel, 0.0) * log_g[:, None, None]), 0.0)
    inner = jnp.einsum('bcnhd,bcmhd->bchnm', q, k) * decay
    inner = jnp.einsum('bchnm,bcmhd->bcnhd', inner, v)
    zeta = jnp.exp((CHUNK - 1 - idx)[:, None] * log_g)
    xi = jnp.exp((idx + 1.0)[:, None] * log_g)
    g_chunk = jnp.exp(CHUNK * log_g)
    kv = jnp.einsum('bcmhd,bcmhe->bchde', k * zeta[:, :, None], v)

    def step(state, inp):
        q_c, kv_c = inp
        cross = jnp.einsum('bnhd,bhde->bnhe', q_c, state)
        return state * g_chunk[:, None, None] + kv_c, cross

    state0 = jnp.zeros((B, RET_HEADS, HEAD_DIM, HEAD_DIM), jnp.float32)
    _, cross = lax.scan(step, state0, (jnp.moveaxis(q * xi[:, :, None], 1, 0), jnp.moveaxis(kv, 1, 0)))
    o = (inner + jnp.moveaxis(cross, 0, 1)).reshape(hs)
    o = _head_norm(o).reshape(B, S, D_BRANCH)
    return (jax.nn.silu(g.astype(jnp.float32)) * o).astype(g.dtype)


def _memory_cross_attention(h, m, wq, wkv, gq, gk, wo):
    B, S, _ = h.shape
    M = m.shape[1]
    q = _rms_norm((h @ wq).reshape(B, S, XA_HEADS, HEAD_DIM), gq)
    k, v = jnp.split(m @ wkv, 2, axis=-1)
    k = _rms_norm(k.reshape(B, M, XA_HEADS, HEAD_DIM), gk)
    v = v.reshape(B, M, XA_HEADS, HEAD_DIM)
    s = jnp.einsum('bshd,bmhd->bhsm', q, k).astype(jnp.float32) * HEAD_DIM ** -0.5
    p = jax.nn.softmax(s, axis=-1).astype(v.dtype)
    o = jnp.einsum('bhsm,bmhd->bshd', p, v).reshape(B, S, D_XA)
    return o @ wo


def _moe_swiglu(h, router, w_gu, w_down):
    B, S, D = h.shape
    t = h.reshape(B * S, D)
    logits = (t @ router).astype(jnp.float32)
    top_val, top_idx = lax.top_k(logits, TOP_K)
    w = jax.nn.softmax(top_val, axis=-1)
    combine = jnp.einsum('tk,tke->te', w, jax.nn.one_hot(top_idx, N_EXPERTS, dtype=jnp.float32)).astype(t.dtype)
    y = jnp.zeros_like(t)
    for e in range(N_EXPERTS):
        y = y + combine[:, e:e + 1] * _swiglu(t, w_gu[e], w_down[e])
    return y.reshape(B, S, D)


def setup_inputs(seed: int = 0) -> dict:
    key = jax.random.key(seed)
    ks = jax.random.split(key, 32)
    nd, nm = (DEPTH + 1) // 2, DEPTH // 2

    def nrm(k, shape, scale):
        return jax.random.normal(k, shape, jnp.float32) * scale

    def gain(k, shape):
        return 1.0 + 0.02 * jax.random.normal(k, shape, jnp.float32)

    x = nrm(ks[0], (BATCH, SEQ, D_MODEL), 1.0)
    mem = nrm(ks[1], (BATCH, N_MEM, D_MODEL), 1.0)
    offset = jax.random.randint(ks[2], (BATCH, 1), 0, 8192, dtype=jnp.int32)
    positions = (offset + jnp.arange(SEQ, dtype=jnp.int32)[None, :]).astype(jnp.int32)
    return {
        'x': x,
        'mem': mem,
        'positions': positions,
        'norm_mix_g': gain(ks[3], (DEPTH, D_MODEL)),
        'w_in': nrm(ks[4], (DEPTH, D_MODEL, D_IN), D_MODEL ** -0.5),
        'conv_w': nrm(ks[5], (DEPTH, CONV_WIDTH, D_CONV), CONV_WIDTH ** -0.5),
        'conv_b': nrm(ks[6], (DEPTH, D_CONV), 0.02),
        'conv_ln_g': gain(ks[7], (DEPTH, D_CONV)),
        'conv_ln_b': nrm(ks[8], (DEPTH, D_CONV), 0.02),
        'pool_w': nrm(ks[9], (DEPTH, len(POOL_WINDOWS), POOL_GROUP, POOL_GROUP), POOL_GROUP ** -0.5),
        'pool_scale': gain(ks[10], (DEPTH, D_POOL)),
        'w_gate': nrm(ks[11], (DEPTH, N_BRANCH, D_MODEL, D_MODEL), D_MODEL ** -0.5),
        'w_branch': nrm(ks[12], (DEPTH, N_BRANCH, D_BRANCH, D_MODEL), D_BRANCH ** -0.5),
        'w_out': nrm(ks[13], (DEPTH, D_MODEL, D_MODEL), D_MODEL ** -0.5),
        'norm_xa_g': gain(ks[14], (DEPTH, D_MODEL)),
        'norm_mem_g': gain(ks[15], (DEPTH, D_MODEL)),
        'xa_wq': nrm(ks[16], (DEPTH, D_MODEL, D_XA), D_MODEL ** -0.5),
        'xa_wkv': nrm(ks[17], (DEPTH, D_MODEL, 2 * D_XA), D_MODEL ** -0.5),
        'xa_q_norm_g': gain(ks[18], (DEPTH, HEAD_DIM)),
        'xa_k_norm_g': gain(ks[19], (DEPTH, HEAD_DIM)),
        'xa_wo': nrm(ks[20], (DEPTH, D_XA, D_MODEL), D_XA ** -0.5),
        'norm_ffn_g': gain(ks[21], (DEPTH, D_MODEL)),
        'ffn_w_gu': nrm(ks[22], (nd, D_MODEL, 2 * D_FF), D_MODEL ** -0.5),
        'ffn_w_down': nrm(ks[23], (nd, D_FF, D_MODEL), D_FF ** -0.5),
        'moe_router': nrm(ks[24], (nm, D_MODEL, N_EXPERTS), D_MODEL ** -0.5),
        'moe_w_gu': nrm(ks[25], (nm, N_EXPERTS, D_MODEL, 2 * D_FF), D_MODEL ** -0.5),
        'moe_w_down': nrm(ks[26], (nm, N_EXPERTS, D_FF, D_MODEL), D_FF ** -0.5),
    }


def reference(x, mem, positions, norm_mix_g, w_in, conv_w, conv_b, conv_ln_g, conv_ln_b,
              pool_w, pool_scale, w_gate, w_branch, w_out, norm_xa_g, norm_mem_g,
              xa_wq, xa_wkv, xa_q_norm_g, xa_k_norm_g, xa_wo, norm_ffn_g,
              ffn_w_gu, ffn_w_down, moe_router, moe_w_gu, moe_w_down):
    B, S, _ = x.shape
    gammas = 1.0 - jnp.exp2(-5.0 - jnp.arange(RET_HEADS, dtype=jnp.float32))
    for l in range(DEPTH):
        h = _rms_norm(x, norm_mix_g[l])
        u_conv, u_pool, u_sb, u_ret = jnp.split(h @ w_in[l], IN_SPLITS, axis=-1)
        y_conv = _conformer_conv(u_conv, conv_w[l], conv_b[l], conv_ln_g[l], conv_ln_b[l])
        y_pool = _multiscale_pool(u_pool, pool_w[l], pool_scale[l])
        q_sb, k_sb, v_sb = [t.reshape(B, S, SB_HEADS, HEAD_DIM) for t in jnp.split(u_sb, 3, axis=-1)]
        y_sb = _stick_breaking(q_sb, k_sb, v_sb).reshape(B, S, D_BRANCH)
        q_r, k_r, v_r, g_r = jnp.split(u_ret, 4, axis=-1)
        y_ret = _retention(q_r, k_r, v_r, g_r, positions, gammas)
        branches = (y_conv, y_pool, y_sb, y_ret)
        merged = jax.nn.sigmoid(h @ w_gate[l, 0]) * (branches[0].astype(x.dtype) @ w_branch[l, 0])
        for i in range(1, N_BRANCH):
            merged = merged + jax.nn.sigmoid(h @ w_gate[l, i]) * (branches[i].astype(x.dtype) @ w_branch[l, i])
        x = x + merged @ w_out[l]
        x = x + _memory_cross_attention(_rms_norm(x, norm_xa_g[l]), _rms_norm(mem, norm_mem_g[l]),
                                        xa_wq[l], xa_wkv[l], xa_q_norm_g[l], xa_k_norm_g[l], xa_wo[l])
        h = _rms_norm(x, norm_ffn_g[l])
        if l % 2 == 0:
            x = x + _swiglu(h, ffn_w_gu[l // 2], ffn_w_down[l // 2])
        else:
            x = x + _moe_swiglu(h, moe_router[l // 2], moe_w_gu[l // 2], moe_w_down[l // 2])
    return x
```

```python
import functools

import jax
import jax.numpy as jnp
from jax import lax
from jax.experimental import pallas as pl
from jax.experimental.pallas import tpu as pltpu

F32 = jnp.float32
BF16 = jnp.bfloat16

D_MODEL = 1024
HEAD_DIM = 64
N_HEADS = 4
D_BRANCH = N_HEADS * HEAD_DIM
CONV_WIDTH = 31
POOL_WINDOWS = (2, 4, 8, 16)
D_FF = 2816
N_EXPERTS = 8
ROPE_THETA = 10000.0
EPS = 1e-6

COL_CONV = 0
COL_POOL = 2
COL_SB_Q, COL_SB_K, COL_SB_V = 3, 4, 5
COL_RET_Q, COL_RET_K, COL_RET_V, COL_RET_G = 6, 7, 8, 9
D_IN = 10 * D_BRANCH

V7X_VMEM_BYTES = 64 * 1024 * 1024
VMEM_LIMIT_BYTES = V7X_VMEM_BYTES - 8 * 1024 * 1024

TOKEN_TILE = 512
SEQ_TILE = 256
CONV_HALO = 32
POOL_HALO = 16
CONV_ROWS = 64
FF_CHUNK = 256
EXPERT_TILE = 512
ROUTER_LANES = 128


def _resident(shape):
    return pl.BlockSpec(shape, lambda *_: (0,) * len(shape), pipeline_mode=pl.Buffered(1))


def _params(*semantics):
    return pltpu.CompilerParams(dimension_semantics=semantics, vmem_limit_bytes=VMEM_LIMIT_BYTES)


def _sigmoid(x):
    return 0.5 * jnp.tanh(0.5 * x) + 0.5


def _silu(x):
    return x * _sigmoid(x)


def _rms_rows(x, g):
    ms = jnp.mean(x * x, axis=-1, keepdims=True)
    return x * lax.rsqrt(ms + EPS) * g


def _dot(a, b):
    return jnp.dot(a, b, preferred_element_type=F32)


def _dot_nt(a, b):
    return lax.dot_general(a, b, (((1,), (1,)), ((), ())), preferred_element_type=F32)


def _dot_tn(a, b):
    return lax.dot_general(a, b, (((0,), (0,)), ((), ())), preferred_element_type=F32)


def _split_dot(x, w):
    hi = x.astype(BF16)
    lo = (x - hi.astype(F32)).astype(BF16)
    return _dot(hi, w) + _dot(lo, w)


def _head_expand(x, lane_head):
    zero = jnp.zeros_like(x)
    return jnp.concatenate([jnp.where(lane_head == h, x, zero) for h in range(N_HEADS)], axis=0)


def _lane_head(width=D_BRANCH):
    return lax.shift_right_logical(lax.broadcasted_iota(jnp.int32, (1, width), 1), 6)


def _group_mean_matrix():
    r = lax.broadcasted_iota(jnp.int32, (D_BRANCH, D_BRANCH), 0)
    c = lax.broadcasted_iota(jnp.int32, (D_BRANCH, D_BRANCH), 1)
    same = lax.shift_right_logical(r, 6) == lax.shift_right_logical(c, 6)
    return jnp.where(same, 1.0 / HEAD_DIM, 0.0).astype(BF16)


def _mix_in_body(x_ref, g_ref, w_ref, u_ref):
    h = _rms_rows(x_ref[...], g_ref[...]).astype(BF16)
    u_ref[...] = _dot(h, w_ref[...]).astype(BF16)


def _mix_in(x2, g, w):
    t = x2.shape[0]
    return pl.pallas_call(
        _mix_in_body,
        grid=(t // TOKEN_TILE,),
        in_specs=[pl.BlockSpec((TOKEN_TILE, D_MODEL), lambda i: (i, 0)),
                  _resident((1, D_MODEL)), _resident((D_MODEL, D_IN))],
        out_specs=pl.BlockSpec((TOKEN_TILE, D_IN), lambda i: (i, 0)),
        out_shape=jax.ShapeDtypeStruct((t, D_IN), BF16),
        compiler_params=_params("parallel"),
        name="mix_in",
    )(x2, g, w)


def _conv_pool_body(uc_ref, uch_ref, up_ref, uph_ref, cw_ref, cb_ref, lg_ref, lb_ref,
                    pw_ref, ps_ref, y_ref, vs_ref, pp_ref, co_ref):
    ts = uc_ref.shape[1]
    i = pl.program_id(1)
    first = i == 0

    uc = uc_ref[0].astype(F32)
    uh = uch_ref[0].astype(F32)
    v_halo = uh[:, :D_BRANCH] * _sigmoid(uh[:, D_BRANCH:])
    vs_ref[0:CONV_HALO, :] = jnp.where(first, 0.0, v_halo)
    vs_ref[CONV_HALO:CONV_HALO + ts, :] = uc[:, :D_BRANCH] * _sigmoid(uc[:, D_BRANCH:])
    lead = CONV_HALO - (CONV_WIDTH - 1)

    for r0 in range(0, ts, CONV_ROWS):
        acc = jnp.zeros((CONV_ROWS, D_BRANCH), F32)
        for k in range(CONV_WIDTH):
            acc = acc + cw_ref[k:k + 1, :] * vs_ref[r0 + lead + k:r0 + lead + k + CONV_ROWS, :]
        co_ref[r0:r0 + CONV_ROWS, :] = acc
    c = co_ref[...] + cb_ref[...]
    mu = jnp.mean(c, axis=-1, keepdims=True)
    cc = c - mu
    var = jnp.mean(cc * cc, axis=-1, keepdims=True)
    y_ref[0, :, 0:D_BRANCH] = _silu(cc * lax.rsqrt(var + EPS) * lg_ref[...] + lb_ref[...]).astype(BF16)

    up = up_ref[0].astype(F32)
    pp_ref[0:POOL_HALO, :] = jnp.where(first, 0.0, uph_ref[0].astype(F32))
    pp_ref[POOL_HALO:POOL_HALO + ts, :] = up
    t_pos = (i * ts + lax.broadcasted_iota(jnp.int32, (ts, 1), 0)).astype(F32) + 1.0
    lane = lax.broadcasted_iota(jnp.int32, (1, 128), 1)
    means = []
    for half, (w_small, w_big) in enumerate(((POOL_WINDOWS[0], POOL_WINDOWS[1]),
                                             (POOL_WINDOWS[2], POOL_WINDOWS[3]))):
        cols = slice(half * 128, (half + 1) * 128)
        s = pp_ref[POOL_HALO:POOL_HALO + ts, cols]
        for j in range(1, w_small):
            s = s + pp_ref[POOL_HALO - j:POOL_HALO - j + ts, cols]
        s_small = s
        for j in range(w_small, w_big):
            s = s + pp_ref[POOL_HALO - j:POOL_HALO - j + ts, cols]
        m_small = s_small / jnp.minimum(t_pos, float(w_small))
        m_big = s / jnp.minimum(t_pos, float(w_big))
        means.append(jnp.where(lane < HEAD_DIM, m_small, m_big))
    d = (jnp.concatenate(means, axis=1) - up).astype(BF16)
    y_ref[0, :, D_BRANCH:2 * D_BRANCH] = (_dot(d, pw_ref[...]) * ps_ref[...]).astype(BF16)


def _conv_pool(u3, conv_w, conv_b, ln_g, ln_b, pool_w_bd, pool_scale):
    b, s, _ = u3.shape
    ts = SEQ_TILE

    def halo(rows, col):
        per = ts // rows
        return lambda bi, i: (bi, jnp.maximum(i * per - 1, 0), col)

    return pl.pallas_call(
        _conv_pool_body,
        grid=(b, s // ts),
        in_specs=[pl.BlockSpec((1, ts, 2 * D_BRANCH), lambda bi, i: (bi, i, COL_CONV)),
                  pl.BlockSpec((1, CONV_HALO, 2 * D_BRANCH), halo(CONV_HALO, COL_CONV)),
                  pl.BlockSpec((1, ts, D_BRANCH), lambda bi, i: (bi, i, COL_POOL)),
                  pl.BlockSpec((1, POOL_HALO, D_BRANCH), halo(POOL_HALO, COL_POOL)),
                  _resident((CONV_HALO, D_BRANCH)), _resident((1, D_BRANCH)),
                  _resident((1, D_BRANCH)), _resident((1, D_BRANCH)),
                  _resident((D_BRANCH, D_BRANCH)), _resident((1, D_BRANCH))],
        out_specs=pl.BlockSpec((1, ts, 2 * D_BRANCH), lambda bi, i: (bi, i, 0)),
        out_shape=jax.ShapeDtypeStruct((b, s, 2 * D_BRANCH), BF16),
        scratch_shapes=[pltpu.VMEM((CONV_HALO + ts, D_BRANCH), F32),
                        pltpu.VMEM((POOL_HALO + ts, D_BRANCH), F32),
                        pltpu.VMEM((ts, D_BRANCH), F32)],
        compiler_params=_params("parallel", "parallel"),
        name="conv_pool",
    )(u3, u3, u3, u3, conv_w, conv_b, ln_g, ln_b, pool_w_bd, pool_scale)


def _softplus(z):
    return jnp.maximum(z, 0.0) + jnp.log(1.0 + jnp.exp(-jnp.abs(z)))


def _stickbreak_body(q_ref, k_ref, v_ref, o_ref):
    tq = q_ref.shape[1]
    tk = tq
    qi = pl.program_id(1)
    q = q_ref[0]
    lane_head = _lane_head()
    row = lax.broadcasted_iota(jnp.int32, (tq, tk), 0)
    col = lax.broadcasted_iota(jnp.int32, (tq, tk), 1)
    tri = (row >= col).astype(BF16)
    scale = HEAD_DIM ** -0.5

    def block(it, carry):
        o = carry[0]
        run = list(carry[1:])
        kb = qi - it
        start = pl.multiple_of(kb * tk, tk)
        kexp = _head_expand(k_ref[0, pl.ds(start, tk), :], lane_head)
        vexp = _head_expand(v_ref[0, pl.ds(start, tk), :], lane_head)
        z_all = _dot_nt(q, kexp) * scale
        mask = (kb * tk + col) < (qi * tq + row)
        probs = []
        for h in range(N_HEADS):
            z = z_all[:, h * tk:(h + 1) * tk]
            sp = jnp.where(mask, _softplus(z), 0.0)
            rev = _dot(sp.astype(BF16), tri)
            probs.append(jnp.where(mask, jnp.exp(z - (rev + run[h])), 0.0).astype(BF16))
            run[h] = run[h] + rev[:, 0:1]
        o = o + _dot(jnp.concatenate(probs, axis=1), vexp)
        return (o, *run)

    init = (jnp.zeros((tq, D_BRANCH), F32),) + tuple(jnp.zeros((tq, 1), F32) for _ in range(N_HEADS))
    out = lax.fori_loop(0, qi + 1, block, init)
    o_ref[0] = out[0].astype(BF16)


def _stickbreak(u3):
    b, s, _ = u3.shape
    tq = SEQ_TILE
    return pl.pallas_call(
        _stickbreak_body,
        grid=(b, s // tq),
        in_specs=[pl.BlockSpec((1, tq, D_BRANCH), lambda bi, i: (bi, i, COL_SB_Q)),
                  pl.BlockSpec((1, s, D_BRANCH), lambda bi, i: (bi, 0, COL_SB_K)),
                  pl.BlockSpec((1, s, D_BRANCH), lambda bi, i: (bi, 0, COL_SB_V))],
        out_specs=pl.BlockSpec((1, tq, D_BRANCH), lambda bi, i: (bi, i, 0)),
        out_shape=jax.ShapeDtypeStruct((b, s, D_BRANCH), BF16),
        compiler_params=_params("parallel", "arbitrary"),
        name="stickbreak",
    )(u3, u3, u3)


def _rope_body(pos_ref, f_ref, cos_ref, sin_ref):
    ang = pos_ref[0].astype(F32) * f_ref[...]
    cos_ref[0] = jnp.cos(ang)
    sin_ref[0] = jnp.sin(ang)


def _rope_tables(positions):
    b, s = positions.shape
    half = HEAD_DIM // 2
    inv_freq = ROPE_THETA ** (-jnp.arange(half, dtype=F32) / half)
    freq = jnp.tile(inv_freq, 128 // half)[None, :]
    ts = SEQ_TILE
    out = jax.ShapeDtypeStruct((b, s, 128), F32)
    return pl.pallas_call(
        _rope_body,
        grid=(b, s // ts),
        in_specs=[pl.BlockSpec((1, ts, 1), lambda bi, i: (bi, i, 0)), _resident((1, 128))],
        out_specs=[pl.BlockSpec((1, ts, 128), lambda bi, i: (bi, i, 0))] * 2,
        out_shape=[out, out],
        compiler_params=_params("parallel", "parallel"),
        name="rope_tables",
    )(positions[:, :, None], freq)


def _retention_body(q_ref, k_ref, v_ref, g_ref, cos_ref, sin_ref, dec_ref, xi_ref, zeta_ref,
                    gam_ref, o_ref, state_ref):
    @pl.when(pl.program_id(1) == 0)
    def _():
        state_ref[...] = jnp.zeros_like(state_ref)

    lane = lax.broadcasted_iota(jnp.int32, (1, D_BRANCH), 1)
    lane_head = lax.shift_right_logical(lane, 6)
    first_half = (lane & (HEAD_DIM - 1)) < HEAD_DIM // 2
    cos = jnp.concatenate([cos_ref[0], cos_ref[0]], axis=1)
    sin = jnp.concatenate([sin_ref[0], sin_ref[0]], axis=1)

    def rope(x):
        partner = jnp.where(first_half, -pltpu.roll(x, D_BRANCH - HEAD_DIM // 2, 1),
                            pltpu.roll(x, HEAD_DIM // 2, 1))
        return x * cos + partner * sin

    q = rope(q_ref[0].astype(F32))
    k = rope(k_ref[0].astype(F32)) * HEAD_DIM ** -0.5
    v = v_ref[0]
    kexp = _head_expand(k.astype(BF16), lane_head)
    vexp = _head_expand(v, lane_head)
    scores = _dot_nt(q.astype(BF16), kexp) * dec_ref[...]
    inner = _dot(scores.astype(BF16), vexp)
    state = state_ref[...]
    cross = _dot((q * xi_ref[...]).astype(BF16), state.astype(BF16))
    update = _dot_tn((k * zeta_ref[...]).astype(BF16), v)
    r = lax.broadcasted_iota(jnp.int32, (D_BRANCH, D_BRANCH), 0)
    c = lax.broadcasted_iota(jnp.int32, (D_BRANCH, D_BRANCH), 1)
    same_head = lax.shift_right_logical(r, 6) == lax.shift_right_logical(c, 6)
    state_ref[...] = state * gam_ref[...] + jnp.where(same_head, update, 0.0)

    o = inner + cross
    gm = _group_mean_matrix()
    oc = o - _split_dot(o, gm)
    var = _split_dot(oc * oc, gm)
    o_ref[0] = (_silu(g_ref[0].astype(F32)) * (oc * lax.rsqrt(var + EPS))).astype(BF16)


def _retention(u3, cos_t, sin_t):
    b, s, _ = u3.shape
    tc = SEQ_TILE
    gammas = 1.0 - jnp.exp2(-5.0 - jnp.arange(N_HEADS, dtype=F32))
    log_g = jnp.log(gammas)
    log_g_lane = jnp.repeat(log_g, HEAD_DIM)[None, :]
    idx = jnp.arange(tc, dtype=F32)
    rel = idx[:, None] - idx[None, :]
    decay = jnp.where(rel >= 0, jnp.exp(jnp.maximum(rel, 0.0) * log_g[:, None, None]), 0.0)
    decay = jnp.transpose(decay, (1, 0, 2)).reshape(tc, N_HEADS * tc)
    xi = jnp.exp((idx + 1.0)[:, None] * log_g_lane)
    zeta = jnp.exp((tc - 1.0 - idx)[:, None] * log_g_lane)
    gam = jnp.broadcast_to(jnp.exp(tc * log_g_lane).T, (D_BRANCH, D_BRANCH))

    def col(cb):
        return pl.BlockSpec((1, tc, D_BRANCH), lambda bi, i: (bi, i, cb))

    tab = pl.BlockSpec((1, tc, 128), lambda bi, i: (bi, i, 0))
    return pl.pallas_call(
        _retention_body,
        grid=(b, s // tc),
        in_specs=[col(COL_RET_Q), col(COL_RET_K), col(COL_RET_V), col(COL_RET_G), tab, tab,
                  _resident((tc, N_HEADS * tc)), _resident((tc, D_BRANCH)),
                  _resident((tc, D_BRANCH)), _resident((D_BRANCH, D_BRANCH))],
        out_specs=pl.BlockSpec((1, tc, D_BRANCH), lambda bi, i: (bi, i, 0)),
        out_shape=jax.ShapeDtypeStruct((b, s, D_BRANCH), BF16),
        scratch_shapes=[pltpu.VMEM((D_BRANCH, D_BRANCH), F32)],
        compiler_params=_params("parallel", "arbitrary"),
        name="retention",
    )(u3, u3, u3, u3, cos_t, sin_t, decay, xi, zeta, gam)


def _merge_body(x_ref, ycp_ref, ysb_ref, yret_ref, g_ref, wg_ref, wb_ref, wo_ref, o_ref):
    x = x_ref[...]
    h = _rms_rows(x, g_ref[...]).astype(BF16)
    ycp = ycp_ref[...]
    branches = (ycp[:, :D_BRANCH], ycp[:, D_BRANCH:], ysb_ref[...], yret_ref[...])
    parts = []
    for n in range(D_MODEL // D_BRANCH):
        cols = slice(n * D_BRANCH, (n + 1) * D_BRANCH)
        m = None
        for i, y in enumerate(branches):
            term = _sigmoid(_dot(h, wg_ref[i, :, cols])) * _dot(y, wb_ref[i, :, cols])
            m = term if m is None else m + term
        parts.append(m.astype(BF16))
    o_ref[...] = x + _dot(jnp.concatenate(parts, axis=1), wo_ref[...])


def _merge(x2, ycp, ysb, yret, g, wg, wb, wo):
    t = x2.shape[0]
    tm = TOKEN_TILE

    def rows(width):
        return pl.BlockSpec((tm, width), lambda i: (i, 0))

    return pl.pallas_call(
        _merge_body,
        grid=(t // tm,),
        in_specs=[rows(D_MODEL), rows(2 * D_BRANCH), rows(D_BRANCH), rows(D_BRANCH),
                  _resident((1, D_MODEL)), _resident((N_HEADS, D_MODEL, D_MODEL)),
                  _resident((N_HEADS, D_BRANCH, D_MODEL)), _resident((D_MODEL, D_MODEL))],
        out_specs=rows(D_MODEL),
        out_shape=jax.ShapeDtypeStruct((t, D_MODEL), F32),
        compiler_params=_params("parallel"),
        name="merge",
    )(x2, ycp, ysb, yret, g, wg, wb, wo)


def _head_rms(x, gain_lanes, gm):
    ms = _split_dot(x * x, gm)
    return x * lax.rsqrt(ms + EPS) * gain_lanes


def _mem_kv_body(m_ref, g_ref, w_ref, gk_ref, k_ref, v_ref):
    hm = _rms_rows(m_ref[0], g_ref[...]).astype(BF16)
    kv = _dot(hm, w_ref[...])
    k_ref[0] = _head_rms(kv[:, :D_BRANCH], gk_ref[...], _group_mean_matrix()).astype(BF16)
    v_ref[0] = kv[:, D_BRANCH:].astype(BF16)


def _mem_kv(mem, g, wkv, gk_lanes):
    b, m, _ = mem.shape
    out = jax.ShapeDtypeStruct((b, m, D_BRANCH), BF16)
    blk = pl.BlockSpec((1, m, D_BRANCH), lambda bi: (bi, 0, 0))
    return pl.pallas_call(
        _mem_kv_body,
        grid=(b,),
        in_specs=[pl.BlockSpec((1, m, D_MODEL), lambda bi: (bi, 0, 0)), _resident((1, D_MODEL)),
                  _resident((D_MODEL, 2 * D_BRANCH)), _resident((1, D_BRANCH))],
        out_specs=[blk, blk],
        out_shape=[out, out],
        compiler_params=_params("parallel"),
        name="mem_kv",
    )(mem, g, wkv, gk_lanes)


def _xattn_body(x_ref, k_ref, v_ref, g_ref, wq_ref, gq_ref, wo_ref, o_ref):
    x = x_ref[0]
    m = k_ref.shape[1]
    h = _rms_rows(x, g_ref[...]).astype(BF16)
    q = _head_rms(_dot(h, wq_ref[...]), gq_ref[...], _group_mean_matrix()).astype(BF16)
    lane_head = _lane_head()
    s_all = _dot_nt(q, _head_expand(k_ref[0], lane_head)) * HEAD_DIM ** -0.5
    probs = []
    for hd in range(N_HEADS):
        s = s_all[:, hd * m:(hd + 1) * m]
        e = jnp.exp(s - jnp.max(s, axis=-1, keepdims=True))
        probs.append((e / jnp.sum(e, axis=-1, keepdims=True)).astype(BF16))
    o = _dot(jnp.concatenate(probs, axis=1), _head_expand(v_ref[0], lane_head))
    o_ref[0] = x + _dot(o.astype(BF16), wo_ref[...])


def _xattn(x3, k, v, g, wq, gq_lanes, wo):
    b, s, _ = x3.shape
    m = k.shape[1]
    ts = TOKEN_TILE
    xblk = pl.BlockSpec((1, ts, D_MODEL), lambda bi, i: (bi, i, 0))
    kvblk = pl.BlockSpec((1, m, D_BRANCH), lambda bi, i: (bi, 0, 0))
    return pl.pallas_call(
        _xattn_body,
        grid=(b, s // ts),
        in_specs=[xblk, kvblk, kvblk, _resident((1, D_MODEL)), _resident((D_MODEL, D_BRANCH)),
                  _resident((1, D_BRANCH)), _resident((D_BRANCH, D_MODEL))],
        out_specs=xblk,
        out_shape=jax.ShapeDtypeStruct((b, s, D_MODEL), F32),
        compiler_params=_params("parallel", "parallel"),
        name="xattn",
    )(x3, k, v, g, wq, gq_lanes, wo)


def _swiglu_rows(h, wg_ref, wu_ref, wd_ref, acc_ref):
    for j in range(D_FF // FF_CHUNK):
        cols = slice(j * FF_CHUNK, (j + 1) * FF_CHUNK)
        a = (_silu(_dot(h, wg_ref[:, cols])) * _dot(h, wu_ref[:, cols])).astype(BF16)
        part = _dot(a, wd_ref[cols, :])
        if j == 0:
            acc_ref[...] = part
        else:
            acc_ref[...] += part


def _ffn_body(x_ref, g_ref, wg_ref, wu_ref, wd_ref, o_ref, acc_ref):
    x = x_ref[...]
    h = _rms_rows(x, g_ref[...]).astype(BF16)
    _swiglu_rows(h, wg_ref, wu_ref, wd_ref, acc_ref)
    o_ref[...] = x + acc_ref[...]


def _ffn(x2, g, wg, wu, wd):
    t = x2.shape[0]
    tm = TOKEN_TILE
    rows = pl.BlockSpec((tm, D_MODEL), lambda i: (i, 0))
    return pl.pallas_call(
        _ffn_body,
        grid=(t // tm,),
        in_specs=[rows, _resident((1, D_MODEL)), _resident((D_MODEL, D_FF)),
                  _resident((D_MODEL, D_FF)), _resident((D_FF, D_MODEL))],
        out_specs=rows,
        out_shape=jax.ShapeDtypeStruct((t, D_MODEL), F32),
        scratch_shapes=[pltpu.VMEM((tm, D_MODEL), F32)],
        compiler_params=_params("parallel"),
        name="ffn",
    )(x2, g, wg, wu, wd)


def _router_body(x_ref, g_ref, rhi_ref, rlo_ref, o_ref):
    h = _rms_rows(x_ref[...], g_ref[...])
    hi = h.astype(BF16)
    lo = (h - hi.astype(F32)).astype(BF16)
    logits = _dot(hi, rhi_ref[...]) + (_dot(hi, rlo_ref[...]) + _dot(lo, rhi_ref[...]))
    lane = lax.broadcasted_iota(jnp.int32, logits.shape, 1)
    neg = jnp.float32(-jnp.inf)
    logits = jnp.where(lane < N_EXPERTS, logits, neg)
    m1 = jnp.max(logits, axis=-1, keepdims=True)
    i1 = jnp.min(jnp.where(logits == m1, lane, ROUTER_LANES), axis=-1, keepdims=True)
    rest = jnp.where(lane == i1, neg, logits)
    m2 = jnp.max(rest, axis=-1, keepdims=True)
    i2 = jnp.min(jnp.where(rest == m2, lane, ROUTER_LANES), axis=-1, keepdims=True)
    e2 = jnp.exp(m2 - m1)
    w1 = 1.0 / (1.0 + e2)
    w2 = e2 / (1.0 + e2)
    out = jnp.where(lane == 0, i1.astype(F32), 0.0)
    out = jnp.where(lane == 1, i2.astype(F32), out)
    out = jnp.where(lane == 2, w1, out)
    o_ref[...] = jnp.where(lane == 3, w2, out)


def _router(x2, g, r_hi, r_lo):
    t = x2.shape[0]
    tm = TOKEN_TILE
    return pl.pallas_call(
        _router_body,
        grid=(t // tm,),
        in_specs=[pl.BlockSpec((tm, D_MODEL), lambda i: (i, 0)), _resident((1, D_MODEL)),
                  _resident((D_MODEL, ROUTER_LANES)), _resident((D_MODEL, ROUTER_LANES))],
        out_specs=pl.BlockSpec((tm, ROUTER_LANES), lambda i: (i, 0)),
        out_shape=jax.ShapeDtypeStruct((t, ROUTER_LANES), F32),
        compiler_params=_params("parallel"),
        name="router",
    )(x2, g, r_hi, r_lo)


def _gather_rows(idx_ref, src_hbm, dst_ref, sem, n_rows):
    def row_copy(r):
        return pltpu.make_async_copy(src_hbm.at[pl.ds(idx_ref[0, 0, r], 1)],
                                     dst_ref.at[pl.ds(r, 1)], sem)

    def start(r, carry):
        row_copy(r).start()
        return carry

    def wait(r, carry):
        row_copy(r).wait()
        return carry

    lax.fori_loop(0, n_rows, start, 0)
    lax.fori_loop(0, n_rows, wait, 0)


def _experts_body(tile_expert_ref, n_tiles_ref, src_ref, x_hbm, sw_ref, g_ref, wg_ref, wu_ref,
                  wd_ref, y_ref, xbuf_ref, acc_ref, sem):
    i = pl.program_id(0)

    @pl.when(i < n_tiles_ref[0])
    def _():
        _gather_rows(src_ref, x_hbm, xbuf_ref, sem, EXPERT_TILE)
        h = _rms_rows(xbuf_ref[...], g_ref[...]).astype(BF16)
        _swiglu_rows(h, wg_ref.at[0], wu_ref.at[0], wd_ref.at[0], acc_ref)
        y_ref[...] = acc_ref[...] * sw_ref[...]

    @pl.when(i >= n_tiles_ref[0])
    def _():
        y_ref[...] = jnp.zeros_like(y_ref)


def _experts(x2, src3, slot_w, tile_expert, n_tiles, g, wg, wu, wd):
    n_steps = src3.shape[0]
    tm = EXPERT_TILE

    def expert(shape):
        return pl.BlockSpec((1,) + shape, lambda i, te, nt: (te[i], 0, 0),
                            pipeline_mode=pl.Buffered(1))

    grid_spec = pltpu.PrefetchScalarGridSpec(
        num_scalar_prefetch=2,
        grid=(n_steps,),
        in_specs=[pl.BlockSpec((1, 1, tm), lambda i, te, nt: (i, 0, 0), memory_space=pltpu.SMEM),
                  pl.BlockSpec(memory_space=pl.ANY),
                  pl.BlockSpec((tm, 1), lambda i, te, nt: (i, 0)),
                  pl.BlockSpec((1, D_MODEL), lambda i, te, nt: (0, 0)),
                  expert((D_MODEL, D_FF)), expert((D_MODEL, D_FF)), expert((D_FF, D_MODEL))],
        out_specs=pl.BlockSpec((tm, D_MODEL), lambda i, te, nt: (i, 0)),
        scratch_shapes=[pltpu.VMEM((tm, D_MODEL), F32), pltpu.VMEM((tm, D_MODEL), F32),
                        pltpu.SemaphoreType.DMA(())],
    )
    return pl.pallas_call(
        _experts_body,
        grid_spec=grid_spec,
        out_shape=jax.ShapeDtypeStruct((n_steps * tm, D_MODEL), F32),
        compiler_params=_params("arbitrary"),
        name="experts",
    )(tile_expert, n_tiles, src3, x2, slot_w, g, wg, wu, wd)


def _combine_body(p0_ref, p1_ref, x_ref, y_hbm, o_ref, b0_ref, b1_ref, sem):
    tm = x_ref.shape[0]

    def copies(r):
        return (pltpu.make_async_copy(y_hbm.at[pl.ds(p0_ref[0, 0, r], 1)], b0_ref.at[pl.ds(r, 1)], sem.at[0]),
                pltpu.make_async_copy(y_hbm.at[pl.ds(p1_ref[0, 0, r], 1)], b1_ref.at[pl.ds(r, 1)], sem.at[1]))

    def start(r, carry):
        for c in copies(r):
            c.start()
        return carry

    def wait(r, carry):
        for c in copies(r):
            c.wait()
        return carry

    lax.fori_loop(0, tm, start, 0)
    lax.fori_loop(0, tm, wait, 0)
    o_ref[...] = x_ref[...] + (b0_ref[...] + b1_ref[...])


def _combine(x2, y, pos0, pos1):
    t = x2.shape[0]
    tm = TOKEN_TILE
    idx = pl.BlockSpec((1, 1, tm), lambda i: (i, 0, 0), memory_space=pltpu.SMEM)
    rows = pl.BlockSpec((tm, D_MODEL), lambda i: (i, 0))
    return pl.pallas_call(
        _combine_body,
        grid=(t // tm,),
        in_specs=[idx, idx, rows, pl.BlockSpec(memory_space=pl.ANY)],
        out_specs=rows,
        out_shape=jax.ShapeDtypeStruct((t, D_MODEL), F32),
        scratch_shapes=[pltpu.VMEM((tm, D_MODEL), F32), pltpu.VMEM((tm, D_MODEL), F32),
                        pltpu.SemaphoreType.DMA((2,))],
        compiler_params=_params("arbitrary"),
        name="combine",
    )(pos0.reshape(t // tm, 1, tm), pos1.reshape(t // tm, 1, tm), x2, y)


def _moe(x2, g, router, wg, wu, wd):
    t = x2.shape[0]
    tm = EXPERT_TILE
    r_pad = jnp.pad(router, ((0, 0), (0, ROUTER_LANES - N_EXPERTS)))
    r_hi = r_pad.astype(BF16)
    r_lo = (r_pad - r_hi.astype(F32)).astype(BF16)
    routed = _router(x2, g, r_hi, r_lo)
    expert_of = routed[:, 0:2].astype(jnp.int32).reshape(-1)
    weight_of = routed[:, 2:4].reshape(-1)

    onehot = (expert_of[:, None] == jnp.arange(N_EXPERTS, dtype=jnp.int32)[None, :]).astype(jnp.int32)
    rank = jnp.sum((jnp.cumsum(onehot, axis=0) - onehot) * onehot, axis=1)
    counts = jnp.sum(onehot, axis=0)
    padded = ((counts + tm - 1) // tm) * tm
    ends = jnp.cumsum(padded)
    pos = (ends - padded)[expert_of] + rank
    n_steps = (2 * t) // tm + N_EXPERTS
    n_slots = n_steps * tm
    src = jnp.zeros((n_slots,), jnp.int32).at[pos].set(jnp.arange(2 * t, dtype=jnp.int32) // 2)
    slot_w = jnp.zeros((n_slots,), F32).at[pos].set(weight_of)
    n_tiles = (ends[-1] // tm).astype(jnp.int32).reshape(1)
    tile_start = jnp.minimum(jnp.arange(n_steps, dtype=jnp.int32), n_tiles[0] - 1) * tm
    tile_expert = jnp.sum((tile_start[:, None] >= ends[None, :]).astype(jnp.int32), axis=1)

    y = _experts(x2, src.reshape(n_steps, 1, tm), slot_w[:, None], tile_expert, n_tiles, g, wg, wu, wd)
    pos2 = pos.reshape(t, 2)
    return _combine(x2, y, pos2[:, 0], pos2[:, 1])


def _block_diag(w):
    g, n, _ = w.shape
    eye = jnp.eye(g, dtype=w.dtype)
    return (eye[:, None, :, None] * w[:, :, None, :]).reshape(g * n, g * n)


def kernel(x, mem, positions, norm_mix_g, w_in, conv_w, conv_b, conv_ln_g, conv_ln_b, pool_w, pool_scale, w_gate, w_branch, w_out, norm_xa_g, norm_mem_g, xa_wq, xa_wkv, xa_q_norm_g, xa_k_norm_g, xa_wo, norm_ffn_g, ffn_w_gu, ffn_w_down, moe_router, moe_w_gu, moe_w_down):
    b, s, d = x.shape
    depth = w_in.shape[0]
    t = b * s
    assert d == D_MODEL and s % TOKEN_TILE == 0 and s % SEQ_TILE == 0

    def row(v):
        return v.astype(F32)[None, :]

    cos_t, sin_t = _rope_tables(positions)
    x2 = x.astype(F32).reshape(t, d)
    for l in range(depth):
        u = _mix_in(x2, row(norm_mix_g[l]), w_in[l].astype(BF16))
        u3 = u.reshape(b, s, D_IN)
        cw = jnp.pad(conv_w[l].astype(F32), ((0, CONV_HALO - CONV_WIDTH), (0, 0)))
        ycp = _conv_pool(u3, cw, row(conv_b[l]), row(conv_ln_g[l]), row(conv_ln_b[l]),
                         _block_diag(pool_w[l]).astype(BF16), row(pool_scale[l]))
        ysb = _stickbreak(u3)
        yret = _retention(u3, cos_t, sin_t)
        x2 = _merge(x2, ycp.reshape(t, 2 * D_BRANCH), ysb.reshape(t, D_BRANCH),
                    yret.reshape(t, D_BRANCH), row(norm_mix_g[l]), w_gate[l].astype(BF16),
                    w_branch[l].astype(BF16), w_out[l].astype(BF16))
        k_mem, v_mem = _mem_kv(mem.astype(F32), row(norm_mem_g[l]), xa_wkv[l].astype(BF16),
                               row(jnp.tile(xa_k_norm_g[l], N_HEADS)))
        x2 = _xattn(x2.reshape(b, s, d), k_mem, v_mem, row(norm_xa_g[l]), xa_wq[l].astype(BF16),
                    row(jnp.tile(xa_q_norm_g[l], N_HEADS)), xa_wo[l].astype(BF16)).reshape(t, d)
        g_ffn = row(norm_ffn_g[l])
        if l % 2 == 0:
            w_gu = ffn_w_gu[l // 2].astype(BF16)
            x2 = _ffn(x2, g_ffn, w_gu[:, :D_FF], w_gu[:, D_FF:], ffn_w_down[l // 2].astype(BF16))
        else:
            w_gu = moe_w_gu[l // 2].astype(BF16)
            x2 = _moe(x2, g_ffn, moe_router[l // 2].astype(F32), w_gu[:, :, :D_FF], w_gu[:, :, D_FF:],
                      moe_w_down[l // 2].astype(BF16))
    return x2.reshape(b, s, d).astype(x.dtype)
```

```python
import functools

import jax
import jax.numpy as jnp
from jax import lax
from jax.experimental import pallas as pl
from jax.experimental.pallas import tpu as pltpu

F32 = jnp.float32
BF16 = jnp.bfloat16

D_MODEL = 1024
HEAD_DIM = 64
N_HEADS = 4
D_BRANCH = N_HEADS * HEAD_DIM
CONV_WIDTH = 31
POOL_WINDOWS = (2, 4, 8, 16)
D_FF = 2816
N_EXPERTS = 8
ROPE_THETA = 10000.0
EPS = 1e-6

COL_CONV = 0
COL_POOL = 2
COL_SB_Q, COL_SB_K, COL_SB_V = 3, 4, 5
COL_RET_Q, COL_RET_K, COL_RET_V, COL_RET_G = 6, 7, 8, 9
D_IN = 10 * D_BRANCH

V7X_VMEM_BYTES = 64 * 1024 * 1024
VMEM_LIMIT_BYTES = V7X_VMEM_BYTES - 8 * 1024 * 1024

TOKEN_TILE = 512
SEQ_TILE = 256
SB_QUERY_TILE = 512
SB_KEY_TILE = 256
SB_CLAMP = 30.0
LOG2E = 1.4426950408889634
CONV_HALO = 32
POOL_HALO = 16
CONV_ROWS = 64
FF_CHUNK = 256
EXPERT_TILE = 512
ROUTER_LANES = 128


def _resident(shape):
    return pl.BlockSpec(shape, lambda *_: (0,) * len(shape), pipeline_mode=pl.Buffered(1))


def _params(*semantics):
    return pltpu.CompilerParams(dimension_semantics=semantics, vmem_limit_bytes=VMEM_LIMIT_BYTES)


def _sigmoid(x):
    return 0.5 * jnp.tanh(0.5 * x) + 0.5


def _silu(x):
    return x * _sigmoid(x)


def _rms_rows(x, g):
    ms = jnp.mean(x * x, axis=-1, keepdims=True)
    return x * lax.rsqrt(ms + EPS) * g


def _dot(a, b):
    return jnp.dot(a, b, preferred_element_type=F32)


def _dot_nt(a, b):
    return lax.dot_general(a, b, (((1,), (1,)), ((), ())), preferred_element_type=F32)


def _dot_tn(a, b):
    return lax.dot_general(a, b, (((0,), (0,)), ((), ())), preferred_element_type=F32)


def _split_dot(x, w):
    hi = x.astype(BF16)
    lo = (x - hi.astype(F32)).astype(BF16)
    return _dot(hi, w) + _dot(lo, w)


def _head_expand(x, lane_head):
    zero = jnp.zeros_like(x)
    return jnp.concatenate([jnp.where(lane_head == h, x, zero) for h in range(N_HEADS)], axis=0)


def _lane_head(width=D_BRANCH):
    return lax.shift_right_logical(lax.broadcasted_iota(jnp.int32, (1, width), 1), 6)


def _group_mean_matrix():
    r = lax.broadcasted_iota(jnp.int32, (D_BRANCH, D_BRANCH), 0)
    c = lax.broadcasted_iota(jnp.int32, (D_BRANCH, D_BRANCH), 1)
    same = lax.shift_right_logical(r, 6) == lax.shift_right_logical(c, 6)
    return jnp.where(same, 1.0 / HEAD_DIM, 0.0).astype(BF16)


def _mix_in_body(x_ref, g_ref, w_ref, u_ref):
    h = _rms_rows(x_ref[...], g_ref[...]).astype(BF16)
    u_ref[...] = _dot(h, w_ref[...]).astype(BF16)


def _mix_in(x2, g, w):
    t = x2.shape[0]
    return pl.pallas_call(
        _mix_in_body,
        grid=(t // TOKEN_TILE,),
        in_specs=[pl.BlockSpec((TOKEN_TILE, D_MODEL), lambda i: (i, 0)),
                  _resident((1, D_MODEL)), _resident((D_MODEL, D_IN))],
        out_specs=pl.BlockSpec((TOKEN_TILE, D_IN), lambda i: (i, 0)),
        out_shape=jax.ShapeDtypeStruct((t, D_IN), BF16),
        compiler_params=_params("parallel"),
        name="mix_in",
    )(x2, g, w)


def _conv_pool_body(uc_ref, uch_ref, up_ref, uph_ref, cw_ref, cb_ref, lg_ref, lb_ref,
                    pw_ref, ps_ref, y_ref, vs_ref, pp_ref, co_ref):
    ts = uc_ref.shape[1]
    i = pl.program_id(1)
    first = i == 0

    uc = uc_ref[0].astype(F32)
    uh = uch_ref[0].astype(F32)
    v_halo = uh[:, :D_BRANCH] * _sigmoid(uh[:, D_BRANCH:])
    vs_ref[0:CONV_HALO, :] = jnp.where(first, 0.0, v_halo)
    vs_ref[CONV_HALO:CONV_HALO + ts, :] = uc[:, :D_BRANCH] * _sigmoid(uc[:, D_BRANCH:])
    lead = CONV_HALO - (CONV_WIDTH - 1)

    for r0 in range(0, ts, CONV_ROWS):
        acc = jnp.zeros((CONV_ROWS, D_BRANCH), F32)
        for k in range(CONV_WIDTH):
            acc = acc + cw_ref[k:k + 1, :] * vs_ref[r0 + lead + k:r0 + lead + k + CONV_ROWS, :]
        co_ref[r0:r0 + CONV_ROWS, :] = acc
    c = co_ref[...] + cb_ref[...]
    mu = jnp.mean(c, axis=-1, keepdims=True)
    cc = c - mu
    var = jnp.mean(cc * cc, axis=-1, keepdims=True)
    y_ref[0, :, 0:D_BRANCH] = _silu(cc * lax.rsqrt(var + EPS) * lg_ref[...] + lb_ref[...]).astype(BF16)

    up = up_ref[0].astype(F32)
    pp_ref[0:POOL_HALO, :] = jnp.where(first, 0.0, uph_ref[0].astype(F32))
    pp_ref[POOL_HALO:POOL_HALO + ts, :] = up
    t_pos = (i * ts + lax.broadcasted_iota(jnp.int32, (ts, 1), 0)).astype(F32) + 1.0
    lane = lax.broadcasted_iota(jnp.int32, (1, 128), 1)
    means = []
    for half, (w_small, w_big) in enumerate(((POOL_WINDOWS[0], POOL_WINDOWS[1]),
                                             (POOL_WINDOWS[2], POOL_WINDOWS[3]))):
        cols = slice(half * 128, (half + 1) * 128)
        s = pp_ref[POOL_HALO:POOL_HALO + ts, cols]
        for j in range(1, w_small):
            s = s + pp_ref[POOL_HALO - j:POOL_HALO - j + ts, cols]
        s_small = s
        for j in range(w_small, w_big):
            s = s + pp_ref[POOL_HALO - j:POOL_HALO - j + ts, cols]
        m_small = s_small / jnp.minimum(t_pos, float(w_small))
        m_big = s / jnp.minimum(t_pos, float(w_big))
        means.append(jnp.where(lane < HEAD_DIM, m_small, m_big))
    d = (jnp.concatenate(means, axis=1) - up).astype(BF16)
    y_ref[0, :, D_BRANCH:2 * D_BRANCH] = (_dot(d, pw_ref[...]) * ps_ref[...]).astype(BF16)


def _conv_pool(u3, conv_w, conv_b, ln_g, ln_b, pool_w_bd, pool_scale):
    b, s, _ = u3.shape
    ts = SEQ_TILE

    def halo(rows, col):
        per = ts // rows
        return lambda bi, i: (bi, jnp.maximum(i * per - 1, 0), col)

    return pl.pallas_call(
        _conv_pool_body,
        grid=(b, s // ts),
        in_specs=[pl.BlockSpec((1, ts, 2 * D_BRANCH), lambda bi, i: (bi, i, COL_CONV)),
                  pl.BlockSpec((1, CONV_HALO, 2 * D_BRANCH), halo(CONV_HALO, COL_CONV)),
                  pl.BlockSpec((1, ts, D_BRANCH), lambda bi, i: (bi, i, COL_POOL)),
                  pl.BlockSpec((1, POOL_HALO, D_BRANCH), halo(POOL_HALO, COL_POOL)),
                  _resident((CONV_HALO, D_BRANCH)), _resident((1, D_BRANCH)),
                  _resident((1, D_BRANCH)), _resident((1, D_BRANCH)),
                  _resident((D_BRANCH, D_BRANCH)), _resident((1, D_BRANCH))],
        out_specs=pl.BlockSpec((1, ts, 2 * D_BRANCH), lambda bi, i: (bi, i, 0)),
        out_shape=jax.ShapeDtypeStruct((b, s, 2 * D_BRANCH), BF16),
        scratch_shapes=[pltpu.VMEM((CONV_HALO + ts, D_BRANCH), F32),
                        pltpu.VMEM((POOL_HALO + ts, D_BRANCH), F32),
                        pltpu.VMEM((ts, D_BRANCH), F32)],
        compiler_params=_params("parallel", "parallel"),
        name="conv_pool",
    )(u3, u3, u3, u3, conv_w, conv_b, ln_g, ln_b, pool_w_bd, pool_scale)


def _stickbreak_body(q_ref, k_ref, v_ref, o_ref, kexp_ref, vexp_ref):
    tq = q_ref.shape[1]
    tk = SB_KEY_TILE
    n_kb = k_ref.shape[1] // tk
    per_q = tq // tk
    qi = pl.program_id(1)
    lane_head = _lane_head()

    @pl.when(qi == 0)
    def _():
        row_head = lax.shift_right_logical(lax.broadcasted_iota(jnp.int32, (D_BRANCH, 1), 0), 6)

        def expand(j, carry):
            start = pl.multiple_of(j * tk, tk)
            kt = k_ref[0, pl.ds(start, tk), :].astype(F32).T
            kexp_ref[j] = jnp.concatenate(
                [jnp.where(row_head == h, kt, 0.0) for h in range(N_HEADS)], axis=1).astype(BF16)
            vexp_ref[j] = _head_expand(v_ref[0, pl.ds(start, tk), :], lane_head)
            return carry

        lax.fori_loop(0, n_kb, expand, 0)

    q = q_ref[0]
    row = lax.broadcasted_iota(jnp.int32, (tk, tk), 0)
    col = lax.broadcasted_iota(jnp.int32, (tk, tk), 1)
    tri = (row >= col).astype(BF16)

    def block(kb, o, neg_run, mask):
        w_all = _dot(q, kexp_ref[kb])
        probs = []
        total = None
        for h in range(N_HEADS):
            w = w_all[:, h * tk:(h + 1) * tk]
            sp = jnp.maximum(jnp.log2(1.0 + jnp.exp2(jnp.minimum(w, SB_CLAMP))), w)
            if mask is not None:
                sp = jnp.where(mask, sp, 0.0)
            rev = _dot(sp.astype(BF16), tri)
            p = jnp.exp2(w - rev)
            if mask is not None:
                p = jnp.where(mask, p, 0.0)
            probs.append(p.astype(BF16))
            total = rev[:, 0:1] if total is None else jnp.where(lane_head == h, rev[:, 0:1], total)
        part = _dot(jnp.concatenate(probs, axis=1), vexp_ref[kb])
        return o + part * jnp.exp2(neg_run), neg_run - total

    o = jnp.zeros((tq, D_BRANCH), F32)
    neg_run = jnp.zeros((tq, D_BRANCH), F32)
    t_loc = lax.broadcasted_iota(jnp.int32, (tq, tk), 0)
    s_loc = lax.broadcasted_iota(jnp.int32, (tq, tk), 1)
    for d in range(per_q - 1, -1, -1):
        o, neg_run = block(qi * per_q + d, o, neg_run, (d * tk + s_loc) < t_loc)

    out = lax.fori_loop(0, qi * per_q, lambda it, c: block(qi * per_q - 1 - it, c[0], c[1], None),
                        (o, neg_run))
    o_ref[0] = out[0].astype(BF16)


def _stickbreak(u3):
    b, s, _ = u3.shape
    tq, tk = SB_QUERY_TILE, SB_KEY_TILE
    return pl.pallas_call(
        _stickbreak_body,
        grid=(b, s // tq),
        in_specs=[pl.BlockSpec((1, tq, D_BRANCH), lambda bi, i: (bi, i, COL_SB_Q)),
                  pl.BlockSpec((1, s, D_BRANCH), lambda bi, i: (bi, 0, COL_SB_K)),
                  pl.BlockSpec((1, s, D_BRANCH), lambda bi, i: (bi, 0, COL_SB_V))],
        out_specs=pl.BlockSpec((1, tq, D_BRANCH), lambda bi, i: (bi, i, 0)),
        out_shape=jax.ShapeDtypeStruct((b, s, D_BRANCH), BF16),
        scratch_shapes=[pltpu.VMEM((s // tk, D_BRANCH, N_HEADS * tk), BF16),
                        pltpu.VMEM((s // tk, N_HEADS * tk, D_BRANCH), BF16)],
        compiler_params=_params("arbitrary", "arbitrary"),
        name="stickbreak",
    )(u3, u3, u3)


def _rope_body(pos_ref, f_ref, cos_ref, sin_ref):
    ang = pos_ref[0].astype(F32) * f_ref[...]
    cos_ref[0] = jnp.cos(ang)
    sin_ref[0] = jnp.sin(ang)


def _rope_tables(positions):
    b, s = positions.shape
    half = HEAD_DIM // 2
    inv_freq = ROPE_THETA ** (-jnp.arange(half, dtype=F32) / half)
    freq = jnp.tile(inv_freq, 128 // half)[None, :]
    ts = SEQ_TILE
    out = jax.ShapeDtypeStruct((b, s, 128), F32)
    return pl.pallas_call(
        _rope_body,
        grid=(b, s // ts),
        in_specs=[pl.BlockSpec((1, ts, 1), lambda bi, i: (bi, i, 0)), _resident((1, 128))],
        out_specs=[pl.BlockSpec((1, ts, 128), lambda bi, i: (bi, i, 0))] * 2,
        out_shape=[out, out],
        compiler_params=_params("parallel", "parallel"),
        name="rope_tables",
    )(positions[:, :, None], freq)


def _retention_body(q_ref, k_ref, v_ref, g_ref, cos_ref, sin_ref, dec_ref, xi_ref, zeta_ref,
                    gam_ref, o_ref, state_ref):
    @pl.when(pl.program_id(1) == 0)
    def _():
        state_ref[...] = jnp.zeros_like(state_ref)

    lane = lax.broadcasted_iota(jnp.int32, (1, D_BRANCH), 1)
    lane_head = lax.shift_right_logical(lane, 6)
    first_half = (lane & (HEAD_DIM - 1)) < HEAD_DIM // 2
    cos = jnp.concatenate([cos_ref[0], cos_ref[0]], axis=1)
    sin = jnp.concatenate([sin_ref[0], sin_ref[0]], axis=1)

    def rope(x):
        partner = jnp.where(first_half, -pltpu.roll(x, D_BRANCH - HEAD_DIM // 2, 1),
                            pltpu.roll(x, HEAD_DIM // 2, 1))
        return x * cos + partner * sin

    q = rope(q_ref[0].astype(F32))
    k = rope(k_ref[0].astype(F32)) * HEAD_DIM ** -0.5
    v = v_ref[0]
    kexp = _head_expand(k.astype(BF16), lane_head)
    vexp = _head_expand(v, lane_head)
    scores = _dot_nt(q.astype(BF16), kexp) * dec_ref[...]
    inner = _dot(scores.astype(BF16), vexp)
    state = state_ref[...]
    cross = _dot((q * xi_ref[...]).astype(BF16), state.astype(BF16))
    update = _dot_tn((k * zeta_ref[...]).astype(BF16), v)
    r = lax.broadcasted_iota(jnp.int32, (D_BRANCH, D_BRANCH), 0)
    c = lax.broadcasted_iota(jnp.int32, (D_BRANCH, D_BRANCH), 1)
    same_head = lax.shift_right_logical(r, 6) == lax.shift_right_logical(c, 6)
    state_ref[...] = state * gam_ref[...] + jnp.where(same_head, update, 0.0)

    o = inner + cross
    gm = _group_mean_matrix()
    oc = o - _split_dot(o, gm)
    var = _split_dot(oc * oc, gm)
    o_ref[0] = (_silu(g_ref[0].astype(F32)) * (oc * lax.rsqrt(var + EPS))).astype(BF16)


def _retention(u3, cos_t, sin_t):
    b, s, _ = u3.shape
    tc = SEQ_TILE
    gammas = 1.0 - jnp.exp2(-5.0 - jnp.arange(N_HEADS, dtype=F32))
    log_g = jnp.log(gammas)
    log_g_lane = jnp.repeat(log_g, HEAD_DIM)[None, :]
    idx = jnp.arange(tc, dtype=F32)
    rel = idx[:, None] - idx[None, :]
    decay = jnp.where(rel >= 0, jnp.exp(jnp.maximum(rel, 0.0) * log_g[:, None, None]), 0.0)
    decay = jnp.transpose(decay, (1, 0, 2)).reshape(tc, N_HEADS * tc)
    xi = jnp.exp((idx + 1.0)[:, None] * log_g_lane)
    zeta = jnp.exp((tc - 1.0 - idx)[:, None] * log_g_lane)
    gam = jnp.broadcast_to(jnp.exp(tc * log_g_lane).T, (D_BRANCH, D_BRANCH))

    def col(cb):
        return pl.BlockSpec((1, tc, D_BRANCH), lambda bi, i: (bi, i, cb))

    tab = pl.BlockSpec((1, tc, 128), lambda bi, i: (bi, i, 0))
    return pl.pallas_call(
        _retention_body,
        grid=(b, s // tc),
        in_specs=[col(COL_RET_Q), col(COL_RET_K), col(COL_RET_V), col(COL_RET_G), tab, tab,
                  _resident((tc, N_HEADS * tc)), _resident((tc, D_BRANCH)),
                  _resident((tc, D_BRANCH)), _resident((D_BRANCH, D_BRANCH))],
        out_specs=pl.BlockSpec((1, tc, D_BRANCH), lambda bi, i: (bi, i, 0)),
        out_shape=jax.ShapeDtypeStruct((b, s, D_BRANCH), BF16),
        scratch_shapes=[pltpu.VMEM((D_BRANCH, D_BRANCH), F32)],
        compiler_params=_params("parallel", "arbitrary"),
        name="retention",
    )(u3, u3, u3, u3, cos_t, sin_t, decay, xi, zeta, gam)


def _merge_body(x_ref, ycp_ref, ysb_ref, yret_ref, g_ref, wg_ref, wb_ref, wo_ref, o_ref):
    x = x_ref[...]
    h = _rms_rows(x, g_ref[...]).astype(BF16)
    ycp = ycp_ref[...]
    branches = (ycp[:, :D_BRANCH], ycp[:, D_BRANCH:], ysb_ref[...], yret_ref[...])
    parts = []
    for n in range(D_MODEL // D_BRANCH):
        cols = slice(n * D_BRANCH, (n + 1) * D_BRANCH)
        m = None
        for i, y in enumerate(branches):
            term = _sigmoid(_dot(h, wg_ref[i, :, cols])) * _dot(y, wb_ref[i, :, cols])
            m = term if m is None else m + term
        parts.append(m.astype(BF16))
    o_ref[...] = x + _dot(jnp.concatenate(parts, axis=1), wo_ref[...])


def _merge(x2, ycp, ysb, yret, g, wg, wb, wo):
    t = x2.shape[0]
    tm = TOKEN_TILE

    def rows(width):
        return pl.BlockSpec((tm, width), lambda i: (i, 0))

    return pl.pallas_call(
        _merge_body,
        grid=(t // tm,),
        in_specs=[rows(D_MODEL), rows(2 * D_BRANCH), rows(D_BRANCH), rows(D_BRANCH),
                  _resident((1, D_MODEL)), _resident((N_HEADS, D_MODEL, D_MODEL)),
                  _resident((N_HEADS, D_BRANCH, D_MODEL)), _resident((D_MODEL, D_MODEL))],
        out_specs=rows(D_MODEL),
        out_shape=jax.ShapeDtypeStruct((t, D_MODEL), F32),
        compiler_params=_params("parallel"),
        name="merge",
    )(x2, ycp, ysb, yret, g, wg, wb, wo)


def _head_rms(x, gain_lanes, gm):
    ms = _split_dot(x * x, gm)
    return x * lax.rsqrt(ms + EPS) * gain_lanes


def _mem_kv_body(m_ref, g_ref, w_ref, gk_ref, k_ref, v_ref):
    hm = _rms_rows(m_ref[0], g_ref[...]).astype(BF16)
    kv = _dot(hm, w_ref[...])
    k_ref[0] = _head_rms(kv[:, :D_BRANCH], gk_ref[...], _group_mean_matrix()).astype(BF16)
    v_ref[0] = kv[:, D_BRANCH:].astype(BF16)


def _mem_kv(mem, g, wkv, gk_lanes):
    b, m, _ = mem.shape
    out = jax.ShapeDtypeStruct((b, m, D_BRANCH), BF16)
    blk = pl.BlockSpec((1, m, D_BRANCH), lambda bi: (bi, 0, 0))
    return pl.pallas_call(
        _mem_kv_body,
        grid=(b,),
        in_specs=[pl.BlockSpec((1, m, D_MODEL), lambda bi: (bi, 0, 0)), _resident((1, D_MODEL)),
                  _resident((D_MODEL, 2 * D_BRANCH)), _resident((1, D_BRANCH))],
        out_specs=[blk, blk],
        out_shape=[out, out],
        compiler_params=_params("parallel"),
        name="mem_kv",
    )(mem, g, wkv, gk_lanes)


def _xattn_body(x_ref, k_ref, v_ref, g_ref, wq_ref, gq_ref, wo_ref, o_ref):
    x = x_ref[0]
    m = k_ref.shape[1]
    h = _rms_rows(x, g_ref[...]).astype(BF16)
    q = _head_rms(_dot(h, wq_ref[...]), gq_ref[...], _group_mean_matrix()).astype(BF16)
    lane_head = _lane_head()
    s_all = _dot_nt(q, _head_expand(k_ref[0], lane_head)) * HEAD_DIM ** -0.5
    probs = []
    for hd in range(N_HEADS):
        s = s_all[:, hd * m:(hd + 1) * m]
        e = jnp.exp(s - jnp.max(s, axis=-1, keepdims=True))
        probs.append((e / jnp.sum(e, axis=-1, keepdims=True)).astype(BF16))
    o = _dot(jnp.concatenate(probs, axis=1), _head_expand(v_ref[0], lane_head))
    o_ref[0] = x + _dot(o.astype(BF16), wo_ref[...])


def _xattn(x3, k, v, g, wq, gq_lanes, wo):
    b, s, _ = x3.shape
    m = k.shape[1]
    ts = TOKEN_TILE
    xblk = pl.BlockSpec((1, ts, D_MODEL), lambda bi, i: (bi, i, 0))
    kvblk = pl.BlockSpec((1, m, D_BRANCH), lambda bi, i: (bi, 0, 0))
    return pl.pallas_call(
        _xattn_body,
        grid=(b, s // ts),
        in_specs=[xblk, kvblk, kvblk, _resident((1, D_MODEL)), _resident((D_MODEL, D_BRANCH)),
                  _resident((1, D_BRANCH)), _resident((D_BRANCH, D_MODEL))],
        out_specs=xblk,
        out_shape=jax.ShapeDtypeStruct((b, s, D_MODEL), F32),
        compiler_params=_params("parallel", "parallel"),
        name="xattn",
    )(x3, k, v, g, wq, gq_lanes, wo)


def _swiglu_rows(h, wg_ref, wu_ref, wd_ref, acc_ref, between=None):
    for j in range(D_FF // FF_CHUNK):
        cols = slice(j * FF_CHUNK, (j + 1) * FF_CHUNK)
        a = (_silu(_dot(h, wg_ref[:, cols])) * _dot(h, wu_ref[:, cols])).astype(BF16)
        part = _dot(a, wd_ref[cols, :])
        if j == 0:
            acc_ref[...] = part
        else:
            acc_ref[...] += part
        if between is not None:
            between(j)


def _ffn_body(x_ref, g_ref, wg_ref, wu_ref, wd_ref, o_ref, acc_ref):
    x = x_ref[...]
    h = _rms_rows(x, g_ref[...]).astype(BF16)
    _swiglu_rows(h, wg_ref, wu_ref, wd_ref, acc_ref)
    o_ref[...] = x + acc_ref[...]


def _ffn(x2, g, wg, wu, wd):
    t = x2.shape[0]
    tm = TOKEN_TILE
    rows = pl.BlockSpec((tm, D_MODEL), lambda i: (i, 0))
    return pl.pallas_call(
        _ffn_body,
        grid=(t // tm,),
        in_specs=[rows, _resident((1, D_MODEL)), _resident((D_MODEL, D_FF)),
                  _resident((D_MODEL, D_FF)), _resident((D_FF, D_MODEL))],
        out_specs=rows,
        out_shape=jax.ShapeDtypeStruct((t, D_MODEL), F32),
        scratch_shapes=[pltpu.VMEM((tm, D_MODEL), F32)],
        compiler_params=_params("parallel"),
        name="ffn",
    )(x2, g, wg, wu, wd)


def _router_body(x_ref, g_ref, rhi_ref, rlo_ref, o_ref):
    h = _rms_rows(x_ref[...], g_ref[...])
    hi = h.astype(BF16)
    lo = (h - hi.astype(F32)).astype(BF16)
    logits = _dot(hi, rhi_ref[...]) + (_dot(hi, rlo_ref[...]) + _dot(lo, rhi_ref[...]))
    lane = lax.broadcasted_iota(jnp.int32, logits.shape, 1)
    neg = jnp.float32(-jnp.inf)
    logits = jnp.where(lane < N_EXPERTS, logits, neg)
    m1 = jnp.max(logits, axis=-1, keepdims=True)
    i1 = jnp.min(jnp.where(logits == m1, lane, ROUTER_LANES), axis=-1, keepdims=True)
    rest = jnp.where(lane == i1, neg, logits)
    m2 = jnp.max(rest, axis=-1, keepdims=True)
    i2 = jnp.min(jnp.where(rest == m2, lane, ROUTER_LANES), axis=-1, keepdims=True)
    e2 = jnp.exp(m2 - m1)
    w1 = 1.0 / (1.0 + e2)
    w2 = e2 / (1.0 + e2)
    out = jnp.where(lane == 0, i1.astype(F32), 0.0)
    out = jnp.where(lane == 1, i2.astype(F32), out)
    out = jnp.where(lane == 2, w1, out)
    o_ref[...] = jnp.where(lane == 3, w2, out)


def _router(x2, g, r_hi, r_lo):
    t = x2.shape[0]
    tm = TOKEN_TILE
    return pl.pallas_call(
        _router_body,
        grid=(t // tm,),
        in_specs=[pl.BlockSpec((tm, D_MODEL), lambda i: (i, 0)), _resident((1, D_MODEL)),
                  _resident((D_MODEL, ROUTER_LANES)), _resident((D_MODEL, ROUTER_LANES))],
        out_specs=pl.BlockSpec((tm, ROUTER_LANES), lambda i: (i, 0)),
        out_shape=jax.ShapeDtypeStruct((t, ROUTER_LANES), F32),
        compiler_params=_params("parallel"),
        name="router",
    )(x2, g, r_hi, r_lo)


def _experts_body(tile_expert_ref, n_tiles_ref, src_ref, src_next_ref, dst_prev_ref, dst_ref,
                  x_hbm, g_ref, wg_ref, wu_ref, wd_ref, y_hbm, xbuf_ref, ybuf_ref, gsem, ssem):
    i = pl.program_id(0)
    last = pl.num_programs(0) - 1
    slot = lax.rem(i, 2)
    other = 1 - slot
    tm = EXPERT_TILE

    def gather(idx_ref, r, s):
        return pltpu.make_async_copy(x_hbm.at[pl.ds(idx_ref[0, 0, r], 1)],
                                     xbuf_ref.at[s, pl.ds(r, 1)], gsem)

    def scatter(idx_ref, r, s):
        return pltpu.make_async_copy(ybuf_ref.at[s, pl.ds(r, 1)],
                                     y_hbm.at[pl.ds(idx_ref[0, 0, r], 1)], ssem)

    def wait_gather():
        pltpu.make_async_copy(x_hbm.at[pl.ds(0, tm)], xbuf_ref.at[0], gsem).wait()

    def wait_scatter():
        pltpu.make_async_copy(ybuf_ref.at[0], y_hbm.at[pl.ds(0, tm)], ssem).wait()

    def start_next(r):
        gather(src_next_ref, r, other).start()
        scatter(dst_prev_ref, r, other).start()

    @pl.when(i == 0)
    def _():
        ybuf_ref[...] = jnp.zeros_like(ybuf_ref)
        lax.fori_loop(0, tm, lambda r, c: (gather(src_ref, r, 0).start(), c)[1], 0)

    wait_gather()

    @pl.when(i > 0)
    def _():
        wait_scatter()

    @pl.when(i < n_tiles_ref[0])
    def _():
        h = _rms_rows(xbuf_ref[slot], g_ref[...]).astype(BF16)
        n_chunks = D_FF // FF_CHUNK
        per = -(-tm // n_chunks)

        def between(j):
            for r in range(j * per, min((j + 1) * per, tm)):
                start_next(r)

        _swiglu_rows(h, wg_ref.at[0], wu_ref.at[0], wd_ref.at[0], ybuf_ref.at[slot], between)

    @pl.when(i >= n_tiles_ref[0])
    def _():
        ybuf_ref[slot] = jnp.zeros((tm, D_MODEL), F32)
        lax.fori_loop(0, tm, lambda r, c: (start_next(r), c)[1], 0)

    @pl.when(i == last)
    def _():
        wait_gather()
        wait_scatter()
        lax.fori_loop(0, tm, lambda r, c: (scatter(dst_ref, r, slot).start(), c)[1], 0)
        wait_scatter()


def _experts(x2, src_tiles, dst_tiles, tile_expert, n_tiles, g, wg, wu, wd):
    n_steps = src_tiles.shape[0] - 1
    tm = EXPERT_TILE

    def idx(off):
        return pl.BlockSpec((1, 1, tm), lambda i, te, nt: (i + off, 0, 0), memory_space=pltpu.SMEM)

    def expert(shape):
        return pl.BlockSpec((1,) + shape, lambda i, te, nt: (te[i], 0, 0),
                            pipeline_mode=pl.Buffered(1))

    grid_spec = pltpu.PrefetchScalarGridSpec(
        num_scalar_prefetch=2,
        grid=(n_steps,),
        in_specs=[idx(0), idx(1), idx(0), idx(1),
                  pl.BlockSpec(memory_space=pl.ANY),
                  pl.BlockSpec((1, D_MODEL), lambda i, te, nt: (0, 0)),
                  expert((D_MODEL, D_FF)), expert((D_MODEL, D_FF)), expert((D_FF, D_MODEL))],
        out_specs=pl.BlockSpec(memory_space=pl.ANY),
        scratch_shapes=[pltpu.VMEM((2, tm, D_MODEL), F32), pltpu.VMEM((2, tm, D_MODEL), F32),
                        pltpu.SemaphoreType.DMA(()), pltpu.SemaphoreType.DMA(())],
    )
    return pl.pallas_call(
        _experts_body,
        grid_spec=grid_spec,
        out_shape=jax.ShapeDtypeStruct(((n_steps + 1) * tm, D_MODEL), F32),
        compiler_params=_params("arbitrary"),
        name="experts",
    )(tile_expert, n_tiles, src_tiles, src_tiles, dst_tiles, dst_tiles, x2, g, wg, wu, wd)


def _combine_body(x_ref, y0_ref, y1_ref, r_ref, o_ref):
    routed = r_ref[...]
    o_ref[...] = x_ref[...] + (routed[:, 2:3] * y0_ref[...] + routed[:, 3:4] * y1_ref[...])


def _combine(x2, y, routed):
    t = x2.shape[0]
    tm = TOKEN_TILE
    rows = pl.BlockSpec((tm, D_MODEL), lambda i: (i, 0))
    return pl.pallas_call(
        _combine_body,
        grid=(t // tm,),
        in_specs=[rows, rows, pl.BlockSpec((tm, D_MODEL), lambda i: (i + t // tm, 0)),
                  pl.BlockSpec((tm, ROUTER_LANES), lambda i: (i, 0))],
        out_specs=rows,
        out_shape=jax.ShapeDtypeStruct((t, D_MODEL), F32),
        compiler_params=_params("parallel"),
        name="combine",
    )(x2, y, y, routed)


def _moe(x2, g, router, wg, wu, wd):
    t = x2.shape[0]
    tm = EXPERT_TILE
    r_pad = jnp.pad(router, ((0, 0), (0, ROUTER_LANES - N_EXPERTS)))
    r_hi = r_pad.astype(BF16)
    r_lo = (r_pad - r_hi.astype(F32)).astype(BF16)
    routed = _router(x2, g, r_hi, r_lo)
    expert_of = routed[:, 0:2].astype(jnp.int32).T.reshape(-1)

    onehot = (expert_of[:, None] == jnp.arange(N_EXPERTS, dtype=jnp.int32)[None, :]).astype(jnp.int32)
    rank = jnp.sum((jnp.cumsum(onehot, axis=0) - onehot) * onehot, axis=1)
    counts = jnp.sum(onehot, axis=0)
    padded = ((counts + tm - 1) // tm) * tm
    ends = jnp.cumsum(padded)
    pos = (ends - padded)[expert_of] + rank
    n_steps = (2 * t) // tm + N_EXPERTS
    n_rows = (n_steps + 1) * tm
    dst = jnp.full((n_rows,), -1, jnp.int32).at[tm + pos].set(
        jnp.arange(2 * t, dtype=jnp.int32), unique_indices=True)
    spare = dst < 0
    dst = jnp.where(spare, 2 * t - 1 + jnp.cumsum(spare.astype(jnp.int32)), dst)
    src = jnp.where(spare, 0, dst % t)
    src = jnp.concatenate([src[tm:], jnp.zeros((tm,), jnp.int32)])
    n_tiles = (ends[-1] // tm).astype(jnp.int32).reshape(1)
    tile_start = jnp.minimum(jnp.arange(n_steps, dtype=jnp.int32), n_tiles[0] - 1) * tm
    tile_expert = jnp.sum((tile_start[:, None] >= ends[None, :]).astype(jnp.int32), axis=1)

    y = _experts(x2, src.reshape(n_steps + 1, 1, tm), dst.reshape(n_steps + 1, 1, tm),
                 tile_expert, n_tiles, g, wg, wu, wd)
    return _combine(x2, y, routed)


def _block_diag(w):
    g, n, _ = w.shape
    eye = jnp.eye(g, dtype=w.dtype)
    return (eye[:, None, :, None] * w[:, :, None, :]).reshape(g * n, g * n)


def kernel(x, mem, positions, norm_mix_g, w_in, conv_w, conv_b, conv_ln_g, conv_ln_b, pool_w, pool_scale, w_gate, w_branch, w_out, norm_xa_g, norm_mem_g, xa_wq, xa_wkv, xa_q_norm_g, xa_k_norm_g, xa_wo, norm_ffn_g, ffn_w_gu, ffn_w_down, moe_router, moe_w_gu, moe_w_down):
    b, s, d = x.shape
    depth = w_in.shape[0]
    t = b * s
    assert d == D_MODEL and s % TOKEN_TILE == 0 and s % SEQ_TILE == 0

    def row(v):
        return v.astype(F32)[None, :]

    cos_t, sin_t = _rope_tables(positions)
    x2 = x.astype(F32).reshape(t, d)
    for l in range(depth):
        sb_q = slice(COL_SB_Q * D_BRANCH, (COL_SB_Q + 1) * D_BRANCH)
        w_in_l = w_in[l].astype(F32).at[:, sb_q].multiply(LOG2E * HEAD_DIM ** -0.5)
        u = _mix_in(x2, row(norm_mix_g[l]), w_in_l.astype(BF16))
        u3 = u.reshape(b, s, D_IN)
        cw = jnp.pad(conv_w[l].astype(F32), ((0, CONV_HALO - CONV_WIDTH), (0, 0)))
        ycp = _conv_pool(u3, cw, row(conv_b[l]), row(conv_ln_g[l]), row(conv_ln_b[l]),
                         _block_diag(pool_w[l]).astype(BF16), row(pool_scale[l]))
        ysb = _stickbreak(u3)
        yret = _retention(u3, cos_t, sin_t)
        x2 = _merge(x2, ycp.reshape(t, 2 * D_BRANCH), ysb.reshape(t, D_BRANCH),
                    yret.reshape(t, D_BRANCH), row(norm_mix_g[l]), w_gate[l].astype(BF16),
                    w_branch[l].astype(BF16), w_out[l].astype(BF16))
        k_mem, v_mem = _mem_kv(mem.astype(F32), row(norm_mem_g[l]), xa_wkv[l].astype(BF16),
                               row(jnp.tile(xa_k_norm_g[l], N_HEADS)))
        x2 = _xattn(x2.reshape(b, s, d), k_mem, v_mem, row(norm_xa_g[l]), xa_wq[l].astype(BF16),
                    row(jnp.tile(xa_q_norm_g[l], N_HEADS)), xa_wo[l].astype(BF16)).reshape(t, d)
        g_ffn = row(norm_ffn_g[l])
        if l % 2 == 0:
            w_gu = ffn_w_gu[l // 2].astype(BF16)
            x2 = _ffn(x2, g_ffn, w_gu[:, :D_FF], w_gu[:, D_FF:], ffn_w_down[l // 2].astype(BF16))
        else:
            w_gu = moe_w_gu[l // 2].astype(BF16)
            x2 = _moe(x2, g_ffn, moe_router[l // 2].astype(F32), w_gu[:, :, :D_FF], w_gu[:, :, D_FF:],
                      moe_w_down[l // 2].astype(BF16))
    return x2.reshape(b, s, d).astype(x.dtype)
```

```python
import functools

import jax
import jax.numpy as jnp
from jax import lax
from jax.experimental import pallas as pl
from jax.experimental.pallas import tpu as pltpu

F32 = jnp.float32
BF16 = jnp.bfloat16

D_MODEL = 1024
HEAD_DIM = 64
N_HEADS = 4
D_BRANCH = N_HEADS * HEAD_DIM
CONV_WIDTH = 31
POOL_WINDOWS = (2, 4, 8, 16)
D_FF = 2816
N_EXPERTS = 8
ROPE_THETA = 10000.0
EPS = 1e-6

COL_CONV = 0
COL_POOL = 2
COL_SB_Q, COL_SB_K, COL_SB_V = 3, 4, 5
COL_RET_Q, COL_RET_K, COL_RET_V, COL_RET_G = 6, 7, 8, 9
D_IN = 10 * D_BRANCH

V7X_VMEM_BYTES = 64 * 1024 * 1024
VMEM_LIMIT_BYTES = V7X_VMEM_BYTES - 8 * 1024 * 1024

TOKEN_TILE = 512
SEQ_TILE = 256
SB_QUERY_TILE = 512
SB_KEY_TILE = 256
SB_CLAMP = 30.0
LOG2E = 1.4426950408889634
SEQ_HALO = 32
CONV_ROWS = 64
FF_CHUNK = 256
EXPERT_TILE = 512
ROUTER_LANES = 128


def _resident(shape):
    return pl.BlockSpec(shape, lambda *_: (0,) * len(shape), pipeline_mode=pl.Buffered(1))


def _params(*semantics):
    return pltpu.CompilerParams(dimension_semantics=semantics, vmem_limit_bytes=VMEM_LIMIT_BYTES)


def _sigmoid(x):
    return 0.5 * jnp.tanh(0.5 * x) + 0.5


def _silu(x):
    return x * _sigmoid(x)


def _rms_rows(x, g):
    ms = jnp.mean(x * x, axis=-1, keepdims=True)
    return x * lax.rsqrt(ms + EPS) * g


def _dot(a, b):
    return jnp.dot(a, b, preferred_element_type=F32)


def _dot_nt(a, b):
    return lax.dot_general(a, b, (((1,), (1,)), ((), ())), preferred_element_type=F32)


def _dot_tn(a, b):
    return lax.dot_general(a, b, (((0,), (0,)), ((), ())), preferred_element_type=F32)


def _split_dot(x, w):
    hi = x.astype(BF16)
    lo = (x - hi.astype(F32)).astype(BF16)
    return _dot(hi, w) + _dot(lo, w)


def _head_expand(x, lane_head):
    zero = jnp.zeros_like(x)
    return jnp.concatenate([jnp.where(lane_head == h, x, zero) for h in range(N_HEADS)], axis=0)


def _lane_head(width=D_BRANCH):
    return lax.shift_right_logical(lax.broadcasted_iota(jnp.int32, (1, width), 1), 6)


def _group_mean_matrix():
    r = lax.broadcasted_iota(jnp.int32, (D_BRANCH, D_BRANCH), 0)
    c = lax.broadcasted_iota(jnp.int32, (D_BRANCH, D_BRANCH), 1)
    same = lax.shift_right_logical(r, 6) == lax.shift_right_logical(c, 6)
    return jnp.where(same, 1.0 / HEAD_DIM, 0.0).astype(BF16)


def _mix_in_body(x_ref, g_ref, w_ref, u_ref):
    h = _rms_rows(x_ref[...], g_ref[...]).astype(BF16)
    u_ref[...] = _dot(h, w_ref[...]).astype(BF16)


def _mix_in(x2, g, w):
    t = x2.shape[0]
    return pl.pallas_call(
        _mix_in_body,
        grid=(t // TOKEN_TILE,),
        in_specs=[pl.BlockSpec((TOKEN_TILE, D_MODEL), lambda i: (i, 0)),
                  _resident((1, D_MODEL)), _resident((D_MODEL, D_IN))],
        out_specs=pl.BlockSpec((TOKEN_TILE, D_IN), lambda i: (i, 0)),
        out_shape=jax.ShapeDtypeStruct((t, D_IN), BF16),
        compiler_params=_params("parallel"),
        name="mix_in",
    )(x2, g, w)


def _conv_pool_body(uc_ref, uch_ref, up_ref, uph_ref, cw_ref, cb_ref, lg_ref, lb_ref,
                    pw_ref, ps_ref, y_ref, vs_ref, sh_ref, co_ref, pp_ref, q_ref):
    ts = uc_ref.shape[1]
    i = pl.program_id(1)
    first = i == 0

    uc = uc_ref[0].astype(F32)
    uh = uch_ref[0].astype(F32)
    v_halo = uh[:, :D_BRANCH] * _sigmoid(uh[:, D_BRANCH:])
    vs_ref[0:SEQ_HALO, :] = jnp.where(first, 0.0, v_halo)
    vs_ref[SEQ_HALO:SEQ_HALO + ts, :] = uc[:, :D_BRANCH] * _sigmoid(uc[:, D_BRANCH:])
    span = ts + SEQ_HALO - 8
    for r in range(1, 8):
        sh_ref[r, 0:span, :] = vs_ref[r:r + span, :]
    lead = SEQ_HALO - (CONV_WIDTH - 1)
    for r0 in range(0, ts, CONV_ROWS):
        acc = jnp.zeros((CONV_ROWS, D_BRANCH), F32)
        for k in range(CONV_WIDTH):
            a, r = divmod(lead + k, 8)
            rows = slice(r0 + 8 * a, r0 + 8 * a + CONV_ROWS)
            tap = vs_ref[rows, :] if r == 0 else sh_ref[r, rows, :]
            acc = acc + cw_ref[k:k + 1, :] * tap
        co_ref[r0:r0 + CONV_ROWS, :] = acc
    c = co_ref[...] + cb_ref[...]
    mu = jnp.mean(c, axis=-1, keepdims=True)
    cc = c - mu
    var = jnp.mean(cc * cc, axis=-1, keepdims=True)
    y_ref[0, :, 0:D_BRANCH] = _silu(cc * lax.rsqrt(var + EPS) * lg_ref[...] + lb_ref[...]).astype(BF16)

    up = up_ref[0].astype(F32)
    pp_ref[0:SEQ_HALO, :] = jnp.where(first, 0.0, uph_ref[0].astype(F32))
    pp_ref[SEQ_HALO:SEQ_HALO + ts, :] = up
    end = SEQ_HALO + ts
    t_pos = (i * ts + lax.broadcasted_iota(jnp.int32, (ts, 1), 0)).astype(F32) + 1.0
    lane = lax.broadcasted_iota(jnp.int32, (1, 128), 1)
    means = []
    for half in range(2):
        cols = slice(half * 128, (half + 1) * 128)
        q_ref[0, 8:end, cols] = pp_ref[8:end, cols] + pp_ref[7:end - 1, cols]
        q_ref[1, 16:end, cols] = q_ref[0, 16:end, cols] + q_ref[0, 14:end - 2, cols]
        if half == 0:
            small, big = q_ref[0, SEQ_HALO:end, cols], q_ref[1, SEQ_HALO:end, cols]
        else:
            q_ref[2, 24:end, cols] = q_ref[1, 24:end, cols] + q_ref[1, 20:end - 4, cols]
            small = q_ref[2, SEQ_HALO:end, cols]
            big = small + q_ref[2, SEQ_HALO - 8:end - 8, cols]
        w_small, w_big = POOL_WINDOWS[2 * half], POOL_WINDOWS[2 * half + 1]
        m_small = small / jnp.minimum(t_pos, float(w_small))
        m_big = big / jnp.minimum(t_pos, float(w_big))
        means.append(jnp.where(lane < HEAD_DIM, m_small, m_big))
    d = (jnp.concatenate(means, axis=1) - up).astype(BF16)
    y_ref[0, :, D_BRANCH:2 * D_BRANCH] = (_dot(d, pw_ref[...]) * ps_ref[...]).astype(BF16)


def _conv_pool(u3, conv_w, conv_b, ln_g, ln_b, pool_w_bd, pool_scale):
    b, s, _ = u3.shape
    ts = SEQ_TILE
    per = ts // SEQ_HALO

    def halo(col):
        return lambda bi, i: (bi, jnp.maximum(i * per - 1, 0), col)

    return pl.pallas_call(
        _conv_pool_body,
        grid=(b, s // ts),
        in_specs=[pl.BlockSpec((1, ts, 2 * D_BRANCH), lambda bi, i: (bi, i, COL_CONV)),
                  pl.BlockSpec((1, SEQ_HALO, 2 * D_BRANCH), halo(COL_CONV)),
                  pl.BlockSpec((1, ts, D_BRANCH), lambda bi, i: (bi, i, COL_POOL)),
                  pl.BlockSpec((1, SEQ_HALO, D_BRANCH), halo(COL_POOL)),
                  _resident((SEQ_HALO, D_BRANCH)), _resident((1, D_BRANCH)),
                  _resident((1, D_BRANCH)), _resident((1, D_BRANCH)),
                  _resident((D_BRANCH, D_BRANCH)), _resident((1, D_BRANCH))],
        out_specs=pl.BlockSpec((1, ts, 2 * D_BRANCH), lambda bi, i: (bi, i, 0)),
        out_shape=jax.ShapeDtypeStruct((b, s, 2 * D_BRANCH), BF16),
        scratch_shapes=[pltpu.VMEM((SEQ_HALO + ts, D_BRANCH), F32),
                        pltpu.VMEM((8, SEQ_HALO + ts, D_BRANCH), F32),
                        pltpu.VMEM((ts, D_BRANCH), F32),
                        pltpu.VMEM((SEQ_HALO + ts, D_BRANCH), F32),
                        pltpu.VMEM((3, SEQ_HALO + ts, D_BRANCH), F32)],
        compiler_params=_params("parallel", "parallel"),
        name="conv_pool",
    )(u3, u3, u3, u3, conv_w, conv_b, ln_g, ln_b, pool_w_bd, pool_scale)


def _stickbreak_body(q_ref, k_ref, v_ref, o_ref, kexp_ref, vexp_ref):
    tq = q_ref.shape[1]
    tk = SB_KEY_TILE
    n_kb = k_ref.shape[1] // tk
    per_q = tq // tk
    qi = pl.program_id(1)
    lane_head = _lane_head()

    @pl.when(qi == 0)
    def _():
        row_head = lax.shift_right_logical(lax.broadcasted_iota(jnp.int32, (D_BRANCH, 1), 0), 6)

        def expand(j, carry):
            start = pl.multiple_of(j * tk, tk)
            kt = k_ref[0, pl.ds(start, tk), :].astype(F32).T
            kexp_ref[j] = jnp.concatenate(
                [jnp.where(row_head == h, kt, 0.0) for h in range(N_HEADS)], axis=1).astype(BF16)
            vexp_ref[j] = _head_expand(v_ref[0, pl.ds(start, tk), :], lane_head)
            return carry

        lax.fori_loop(0, n_kb, expand, 0)

    q = q_ref[0]
    row = lax.broadcasted_iota(jnp.int32, (tk, tk), 0)
    col = lax.broadcasted_iota(jnp.int32, (tk, tk), 1)
    tri = (row >= col).astype(BF16)

    def block(kb, o, neg_run, mask):
        w_all = _dot(q, kexp_ref[kb])
        probs = []
        total = None
        for h in range(N_HEADS):
            w = w_all[:, h * tk:(h + 1) * tk]
            sp = jnp.maximum(jnp.log2(1.0 + jnp.exp2(jnp.minimum(w, SB_CLAMP))), w)
            if mask is not None:
                sp = jnp.where(mask, sp, 0.0)
            rev = _dot(sp.astype(BF16), tri)
            p = jnp.exp2(w - rev)
            if mask is not None:
                p = jnp.where(mask, p, 0.0)
            probs.append(p.astype(BF16))
            total = rev[:, 0:1] if total is None else jnp.where(lane_head == h, rev[:, 0:1], total)
        part = _dot(jnp.concatenate(probs, axis=1), vexp_ref[kb])
        return o + part * jnp.exp2(neg_run), neg_run - total

    o = jnp.zeros((tq, D_BRANCH), F32)
    neg_run = jnp.zeros((tq, D_BRANCH), F32)
    t_loc = lax.broadcasted_iota(jnp.int32, (tq, tk), 0)
    s_loc = lax.broadcasted_iota(jnp.int32, (tq, tk), 1)
    for d in range(per_q - 1, -1, -1):
        o, neg_run = block(qi * per_q + d, o, neg_run, (d * tk + s_loc) < t_loc)

    out = lax.fori_loop(0, qi * per_q, lambda it, c: block(qi * per_q - 1 - it, c[0], c[1], None),
                        (o, neg_run))
    o_ref[0] = out[0].astype(BF16)


def _stickbreak(u3):
    b, s, _ = u3.shape
    tq, tk = SB_QUERY_TILE, SB_KEY_TILE
    return pl.pallas_call(
        _stickbreak_body,
        grid=(b, s // tq),
        in_specs=[pl.BlockSpec((1, tq, D_BRANCH), lambda bi, i: (bi, i, COL_SB_Q)),
                  pl.BlockSpec((1, s, D_BRANCH), lambda bi, i: (bi, 0, COL_SB_K)),
                  pl.BlockSpec((1, s, D_BRANCH), lambda bi, i: (bi, 0, COL_SB_V))],
        out_specs=pl.BlockSpec((1, tq, D_BRANCH), lambda bi, i: (bi, i, 0)),
        out_shape=jax.ShapeDtypeStruct((b, s, D_BRANCH), BF16),
        scratch_shapes=[pltpu.VMEM((s // tk, D_BRANCH, N_HEADS * tk), BF16),
                        pltpu.VMEM((s // tk, N_HEADS * tk, D_BRANCH), BF16)],
        compiler_params=_params("arbitrary", "arbitrary"),
        name="stickbreak",
    )(u3, u3, u3)


def _rope_body(pos_ref, f_ref, cos_ref, sin_ref):
    ang = pos_ref[0].astype(F32) * f_ref[...]
    cos_ref[0] = jnp.cos(ang)
    sin_ref[0] = jnp.sin(ang)


def _rope_tables(positions):
    b, s = positions.shape
    half = HEAD_DIM // 2
    inv_freq = ROPE_THETA ** (-jnp.arange(half, dtype=F32) / half)
    freq = jnp.tile(inv_freq, 128 // half)[None, :]
    ts = SEQ_TILE
    out = jax.ShapeDtypeStruct((b, s, 128), F32)
    return pl.pallas_call(
        _rope_body,
        grid=(b, s // ts),
        in_specs=[pl.BlockSpec((1, ts, 1), lambda bi, i: (bi, i, 0)), _resident((1, 128))],
        out_specs=[pl.BlockSpec((1, ts, 128), lambda bi, i: (bi, i, 0))] * 2,
        out_shape=[out, out],
        compiler_params=_params("parallel", "parallel"),
        name="rope_tables",
    )(positions[:, :, None], freq)


def _retention_body(q_ref, k_ref, v_ref, g_ref, cos_ref, sin_ref, dec_ref, xi_ref, zeta_ref,
                    gam_ref, o_ref, state_ref):
    @pl.when(pl.program_id(1) == 0)
    def _():
        state_ref[...] = jnp.zeros_like(state_ref)

    lane = lax.broadcasted_iota(jnp.int32, (1, D_BRANCH), 1)
    lane_head = lax.shift_right_logical(lane, 6)
    first_half = (lane & (HEAD_DIM - 1)) < HEAD_DIM // 2
    cos = jnp.concatenate([cos_ref[0], cos_ref[0]], axis=1)
    sin = jnp.concatenate([sin_ref[0], sin_ref[0]], axis=1)

    def rope(x):
        partner = jnp.where(first_half, -pltpu.roll(x, D_BRANCH - HEAD_DIM // 2, 1),
                            pltpu.roll(x, HEAD_DIM // 2, 1))
        return x * cos + partner * sin

    q = rope(q_ref[0].astype(F32))
    k = rope(k_ref[0].astype(F32)) * HEAD_DIM ** -0.5
    v = v_ref[0]
    kexp = _head_expand(k.astype(BF16), lane_head)
    vexp = _head_expand(v, lane_head)
    scores = _dot_nt(q.astype(BF16), kexp) * dec_ref[...]
    inner = _dot(scores.astype(BF16), vexp)
    state = state_ref[...]
    cross = _dot((q * xi_ref[...]).astype(BF16), state.astype(BF16))
    update = _dot_tn((k * zeta_ref[...]).astype(BF16), v)
    r = lax.broadcasted_iota(jnp.int32, (D_BRANCH, D_BRANCH), 0)
    c = lax.broadcasted_iota(jnp.int32, (D_BRANCH, D_BRANCH), 1)
    same_head = lax.shift_right_logical(r, 6) == lax.shift_right_logical(c, 6)
    state_ref[...] = state * gam_ref[...] + jnp.where(same_head, update, 0.0)

    o = inner + cross
    gm = _group_mean_matrix()
    oc = o - _split_dot(o, gm)
    var = _split_dot(oc * oc, gm)
    o_ref[0] = (_silu(g_ref[0].astype(F32)) * (oc * lax.rsqrt(var + EPS))).astype(BF16)


def _retention(u3, cos_t, sin_t):
    b, s, _ = u3.shape
    tc = SEQ_TILE
    gammas = 1.0 - jnp.exp2(-5.0 - jnp.arange(N_HEADS, dtype=F32))
    log_g = jnp.log(gammas)
    log_g_lane = jnp.repeat(log_g, HEAD_DIM)[None, :]
    idx = jnp.arange(tc, dtype=F32)
    rel = idx[:, None] - idx[None, :]
    decay = jnp.where(rel >= 0, jnp.exp(jnp.maximum(rel, 0.0) * log_g[:, None, None]), 0.0)
    decay = jnp.transpose(decay, (1, 0, 2)).reshape(tc, N_HEADS * tc)
    xi = jnp.exp((idx + 1.0)[:, None] * log_g_lane)
    zeta = jnp.exp((tc - 1.0 - idx)[:, None] * log_g_lane)
    gam = jnp.broadcast_to(jnp.exp(tc * log_g_lane).T, (D_BRANCH, D_BRANCH))

    def col(cb):
        return pl.BlockSpec((1, tc, D_BRANCH), lambda bi, i: (bi, i, cb))

    tab = pl.BlockSpec((1, tc, 128), lambda bi, i: (bi, i, 0))
    return pl.pallas_call(
        _retention_body,
        grid=(b, s // tc),
        in_specs=[col(COL_RET_Q), col(COL_RET_K), col(COL_RET_V), col(COL_RET_G), tab, tab,
                  _resident((tc, N_HEADS * tc)), _resident((tc, D_BRANCH)),
                  _resident((tc, D_BRANCH)), _resident((D_BRANCH, D_BRANCH))],
        out_specs=pl.BlockSpec((1, tc, D_BRANCH), lambda bi, i: (bi, i, 0)),
        out_shape=jax.ShapeDtypeStruct((b, s, D_BRANCH), BF16),
        scratch_shapes=[pltpu.VMEM((D_BRANCH, D_BRANCH), F32)],
        compiler_params=_params("parallel", "arbitrary"),
        name="retention",
    )(u3, u3, u3, u3, cos_t, sin_t, decay, xi, zeta, gam)


def _merge_body(x_ref, ycp_ref, ysb_ref, yret_ref, g_ref, wg_ref, wb_ref, wo_ref, o_ref):
    x = x_ref[...]
    h = _rms_rows(x, g_ref[...]).astype(BF16)
    ycp = ycp_ref[...]
    branches = (ycp[:, :D_BRANCH], ycp[:, D_BRANCH:], ysb_ref[...], yret_ref[...])
    parts = []
    for n in range(D_MODEL // D_BRANCH):
        cols = slice(n * D_BRANCH, (n + 1) * D_BRANCH)
        m = None
        for i, y in enumerate(branches):
            term = _sigmoid(_dot(h, wg_ref[i, :, cols])) * _dot(y, wb_ref[i, :, cols])
            m = term if m is None else m + term
        parts.append(m.astype(BF16))
    o_ref[...] = x + _dot(jnp.concatenate(parts, axis=1), wo_ref[...])


def _merge(x2, ycp, ysb, yret, g, wg, wb, wo):
    t = x2.shape[0]
    tm = TOKEN_TILE

    def rows(width):
        return pl.BlockSpec((tm, width), lambda i: (i, 0))

    return pl.pallas_call(
        _merge_body,
        grid=(t // tm,),
        in_specs=[rows(D_MODEL), rows(2 * D_BRANCH), rows(D_BRANCH), rows(D_BRANCH),
                  _resident((1, D_MODEL)), _resident((N_HEADS, D_MODEL, D_MODEL)),
                  _resident((N_HEADS, D_BRANCH, D_MODEL)), _resident((D_MODEL, D_MODEL))],
        out_specs=rows(D_MODEL),
        out_shape=jax.ShapeDtypeStruct((t, D_MODEL), F32),
        compiler_params=_params("parallel"),
        name="merge",
    )(x2, ycp, ysb, yret, g, wg, wb, wo)


def _head_rms(x, gain_lanes, gm):
    ms = _split_dot(x * x, gm)
    return x * lax.rsqrt(ms + EPS) * gain_lanes


def _mem_kv_body(m_ref, g_ref, w_ref, gk_ref, k_ref, v_ref):
    hm = _rms_rows(m_ref[0], g_ref[...]).astype(BF16)
    kv = _dot(hm, w_ref[...])
    k_ref[0] = _head_rms(kv[:, :D_BRANCH], gk_ref[...], _group_mean_matrix()).astype(BF16)
    v_ref[0] = kv[:, D_BRANCH:].astype(BF16)


def _mem_kv(mem, g, wkv, gk_lanes):
    b, m, _ = mem.shape
    out = jax.ShapeDtypeStruct((b, m, D_BRANCH), BF16)
    blk = pl.BlockSpec((1, m, D_BRANCH), lambda bi: (bi, 0, 0))
    return pl.pallas_call(
        _mem_kv_body,
        grid=(b,),
        in_specs=[pl.BlockSpec((1, m, D_MODEL), lambda bi: (bi, 0, 0)), _resident((1, D_MODEL)),
                  _resident((D_MODEL, 2 * D_BRANCH)), _resident((1, D_BRANCH))],
        out_specs=[blk, blk],
        out_shape=[out, out],
        compiler_params=_params("parallel"),
        name="mem_kv",
    )(mem, g, wkv, gk_lanes)


def _xattn_body(x_ref, k_ref, v_ref, g_ref, wq_ref, gq_ref, wo_ref, o_ref):
    x = x_ref[0]
    m = k_ref.shape[1]
    h = _rms_rows(x, g_ref[...]).astype(BF16)
    q = _head_rms(_dot(h, wq_ref[...]), gq_ref[...], _group_mean_matrix()).astype(BF16)
    lane_head = _lane_head()
    s_all = _dot_nt(q, _head_expand(k_ref[0], lane_head)) * HEAD_DIM ** -0.5
    probs = []
    for hd in range(N_HEADS):
        s = s_all[:, hd * m:(hd + 1) * m]
        e = jnp.exp(s - jnp.max(s, axis=-1, keepdims=True))
        probs.append((e / jnp.sum(e, axis=-1, keepdims=True)).astype(BF16))
    o = _dot(jnp.concatenate(probs, axis=1), _head_expand(v_ref[0], lane_head))
    o_ref[0] = x + _dot(o.astype(BF16), wo_ref[...])


def _xattn(x3, k, v, g, wq, gq_lanes, wo):
    b, s, _ = x3.shape
    m = k.shape[1]
    ts = TOKEN_TILE
    xblk = pl.BlockSpec((1, ts, D_MODEL), lambda bi, i: (bi, i, 0))
    kvblk = pl.BlockSpec((1, m, D_BRANCH), lambda bi, i: (bi, 0, 0))
    return pl.pallas_call(
        _xattn_body,
        grid=(b, s // ts),
        in_specs=[xblk, kvblk, kvblk, _resident((1, D_MODEL)), _resident((D_MODEL, D_BRANCH)),
                  _resident((1, D_BRANCH)), _resident((D_BRANCH, D_MODEL))],
        out_specs=xblk,
        out_shape=jax.ShapeDtypeStruct((b, s, D_MODEL), F32),
        compiler_params=_params("parallel", "parallel"),
        name="xattn",
    )(x3, k, v, g, wq, gq_lanes, wo)


def _swiglu_rows(h, wg_ref, wu_ref, wd_ref, acc_ref, between=None):
    for j in range(D_FF // FF_CHUNK):
        cols = slice(j * FF_CHUNK, (j + 1) * FF_CHUNK)
        a = (_silu(_dot(h, wg_ref[:, cols])) * _dot(h, wu_ref[:, cols])).astype(BF16)
        part = _dot(a, wd_ref[cols, :])
        if j == 0:
            acc_ref[...] = part
        else:
            acc_ref[...] += part
        if between is not None:
            between(j)


def _ffn_body(x_ref, g_ref, wg_ref, wu_ref, wd_ref, o_ref, acc_ref):
    x = x_ref[...]
    h = _rms_rows(x, g_ref[...]).astype(BF16)
    _swiglu_rows(h, wg_ref, wu_ref, wd_ref, acc_ref)
    o_ref[...] = x + acc_ref[...]


def _ffn(x2, g, wg, wu, wd):
    t = x2.shape[0]
    tm = TOKEN_TILE
    rows = pl.BlockSpec((tm, D_MODEL), lambda i: (i, 0))
    return pl.pallas_call(
        _ffn_body,
        grid=(t // tm,),
        in_specs=[rows, _resident((1, D_MODEL)), _resident((D_MODEL, D_FF)),
                  _resident((D_MODEL, D_FF)), _resident((D_FF, D_MODEL))],
        out_specs=rows,
        out_shape=jax.ShapeDtypeStruct((t, D_MODEL), F32),
        scratch_shapes=[pltpu.VMEM((tm, D_MODEL), F32)],
        compiler_params=_params("parallel"),
        name="ffn",
    )(x2, g, wg, wu, wd)


LANE_TILES = D_MODEL // 128


def _to_token_tiles(x, o_ref):
    n = x.shape[0]
    for c in range(LANE_TILES):
        o_ref[pl.ds(c, n, stride=LANE_TILES), :] = x[:, c * 128:(c + 1) * 128]


def _from_token_tiles(ref, n):
    return jnp.concatenate([ref[pl.ds(c, n, stride=LANE_TILES), :] for c in range(LANE_TILES)], axis=1)


def _router_body(x_ref, g_ref, rhi_ref, rlo_ref, o_ref, xt_ref):
    x = x_ref[...]
    _to_token_tiles(x, xt_ref)
    h = _rms_rows(x, g_ref[...])
    hi = h.astype(BF16)
    lo = (h - hi.astype(F32)).astype(BF16)
    logits = _dot(hi, rhi_ref[...]) + (_dot(hi, rlo_ref[...]) + _dot(lo, rhi_ref[...]))
    lane = lax.broadcasted_iota(jnp.int32, logits.shape, 1)
    neg = jnp.float32(-jnp.inf)
    logits = jnp.where(lane < N_EXPERTS, logits, neg)
    m1 = jnp.max(logits, axis=-1, keepdims=True)
    i1 = jnp.min(jnp.where(logits == m1, lane, ROUTER_LANES), axis=-1, keepdims=True)
    rest = jnp.where(lane == i1, neg, logits)
    m2 = jnp.max(rest, axis=-1, keepdims=True)
    i2 = jnp.min(jnp.where(rest == m2, lane, ROUTER_LANES), axis=-1, keepdims=True)
    e2 = jnp.exp(m2 - m1)
    w1 = 1.0 / (1.0 + e2)
    w2 = e2 / (1.0 + e2)
    out = jnp.where(lane == 0, i1.astype(F32), 0.0)
    out = jnp.where(lane == 1, i2.astype(F32), out)
    out = jnp.where(lane == 2, w1, out)
    o_ref[...] = jnp.where(lane == 3, w2, out)


def _router(x2, g, r_hi, r_lo):
    t = x2.shape[0]
    tm = TOKEN_TILE
    return pl.pallas_call(
        _router_body,
        grid=(t // tm,),
        in_specs=[pl.BlockSpec((tm, D_MODEL), lambda i: (i, 0)), _resident((1, D_MODEL)),
                  _resident((D_MODEL, ROUTER_LANES)), _resident((D_MODEL, ROUTER_LANES))],
        out_specs=[pl.BlockSpec((tm, ROUTER_LANES), lambda i: (i, 0)),
                   pl.BlockSpec((tm * LANE_TILES, 128), lambda i: (i, 0))],
        out_shape=[jax.ShapeDtypeStruct((t, ROUTER_LANES), F32),
                   jax.ShapeDtypeStruct((t * LANE_TILES, 128), F32)],
        compiler_params=_params("parallel"),
        name="router",
    )(x2, g, r_hi, r_lo)


def _experts_body(tile_expert_ref, n_tiles_ref, src_ref, src_next_ref, dst_prev_ref, dst_ref,
                  x_hbm, g_ref, wg_ref, wu_ref, wd_ref, y_hbm, xbuf_ref, ybuf_ref, acc_ref, gsem, ssem):
    i = pl.program_id(0)
    last = pl.num_programs(0) - 1
    slot = lax.rem(i, 2)
    other = 1 - slot
    tm = EXPERT_TILE
    rows = LANE_TILES

    def gather(idx_ref, r, s):
        src = pl.multiple_of(idx_ref[0, 0, r] * rows, rows)
        return pltpu.make_async_copy(x_hbm.at[pl.ds(src, rows)],
                                     xbuf_ref.at[s, pl.ds(r * rows, rows)], gsem)

    def scatter(idx_ref, r, s):
        dst = pl.multiple_of(idx_ref[0, 0, r] * rows, rows)
        return pltpu.make_async_copy(ybuf_ref.at[s, pl.ds(r * rows, rows)],
                                     y_hbm.at[pl.ds(dst, rows)], ssem)

    def wait_gather():
        pltpu.make_async_copy(x_hbm.at[pl.ds(0, tm * rows)], xbuf_ref.at[0], gsem).wait()

    def wait_scatter():
        pltpu.make_async_copy(ybuf_ref.at[0], y_hbm.at[pl.ds(0, tm * rows)], ssem).wait()

    def start_next(r):
        gather(src_next_ref, r, other).start()
        scatter(dst_prev_ref, r, other).start()

    @pl.when(i == 0)
    def _():
        ybuf_ref[...] = jnp.zeros_like(ybuf_ref)
        lax.fori_loop(0, tm, lambda r, c: (gather(src_ref, r, 0).start(), c)[1], 0)

    wait_gather()

    @pl.when(i > 0)
    def _():
        wait_scatter()

    @pl.when(i < n_tiles_ref[0])
    def _():
        h = _rms_rows(_from_token_tiles(xbuf_ref.at[slot], tm), g_ref[...]).astype(BF16)
        n_chunks = D_FF // FF_CHUNK
        per = -(-tm // n_chunks)

        def between(j):
            for r in range(j * per, min((j + 1) * per, tm)):
                start_next(r)

        _swiglu_rows(h, wg_ref.at[0], wu_ref.at[0], wd_ref.at[0], acc_ref, between)
        _to_token_tiles(acc_ref[...], ybuf_ref.at[slot])

    @pl.when(i >= n_tiles_ref[0])
    def _():
        ybuf_ref[slot] = jnp.zeros((tm * rows, 128), F32)
        lax.fori_loop(0, tm, lambda r, c: (start_next(r), c)[1], 0)

    @pl.when(i == last)
    def _():
        wait_gather()
        wait_scatter()
        lax.fori_loop(0, tm, lambda r, c: (scatter(dst_ref, r, slot).start(), c)[1], 0)
        wait_scatter()


def _experts(x_tiles, src_tiles, dst_tiles, tile_expert, n_tiles, g, wg, wu, wd):
    n_steps = src_tiles.shape[0] - 1
    tm = EXPERT_TILE

    def idx(off):
        return pl.BlockSpec((1, 1, tm), lambda i, te, nt: (i + off, 0, 0), memory_space=pltpu.SMEM)

    def expert(shape):
        return pl.BlockSpec((1,) + shape, lambda i, te, nt: (te[i], 0, 0),
                            pipeline_mode=pl.Buffered(1))

    grid_spec = pltpu.PrefetchScalarGridSpec(
        num_scalar_prefetch=2,
        grid=(n_steps,),
        in_specs=[idx(0), idx(1), idx(0), idx(1),
                  pl.BlockSpec(memory_space=pl.ANY),
                  pl.BlockSpec((1, D_MODEL), lambda i, te, nt: (0, 0)),
                  expert((D_MODEL, D_FF)), expert((D_MODEL, D_FF)), expert((D_FF, D_MODEL))],
        out_specs=pl.BlockSpec(memory_space=pl.ANY),
        scratch_shapes=[pltpu.VMEM((2, tm * LANE_TILES, 128), F32),
                        pltpu.VMEM((2, tm * LANE_TILES, 128), F32),
                        pltpu.VMEM((tm, D_MODEL), F32),
                        pltpu.SemaphoreType.DMA(()), pltpu.SemaphoreType.DMA(())],
    )
    return pl.pallas_call(
        _experts_body,
        grid_spec=grid_spec,
        out_shape=jax.ShapeDtypeStruct(((n_steps + 1) * tm * LANE_TILES, 128), F32),
        compiler_params=_params("arbitrary"),
        name="experts",
    )(tile_expert, n_tiles, src_tiles, src_tiles, dst_tiles, dst_tiles, x_tiles, g, wg, wu, wd)


def _combine_body(x_ref, y0_ref, y1_ref, r_ref, o_ref):
    routed = r_ref[...]
    tm = x_ref.shape[0]
    y0 = _from_token_tiles(y0_ref, tm)
    y1 = _from_token_tiles(y1_ref, tm)
    o_ref[...] = x_ref[...] + (routed[:, 2:3] * y0 + routed[:, 3:4] * y1)


def _combine(x2, y_tiles, routed):
    t = x2.shape[0]
    tm = TOKEN_TILE
    rows = pl.BlockSpec((tm, D_MODEL), lambda i: (i, 0))
    return pl.pallas_call(
        _combine_body,
        grid=(t // tm,),
        in_specs=[rows, pl.BlockSpec((tm * LANE_TILES, 128), lambda i: (i, 0)),
                  pl.BlockSpec((tm * LANE_TILES, 128), lambda i: (i + t // tm, 0)),
                  pl.BlockSpec((tm, ROUTER_LANES), lambda i: (i, 0))],
        out_specs=rows,
        out_shape=jax.ShapeDtypeStruct((t, D_MODEL), F32),
        compiler_params=_params("parallel"),
        name="combine",
    )(x2, y_tiles, y_tiles, routed)


def _moe(x2, g, router, wg, wu, wd):
    t = x2.shape[0]
    tm = EXPERT_TILE
    r_pad = jnp.pad(router, ((0, 0), (0, ROUTER_LANES - N_EXPERTS)))
    r_hi = r_pad.astype(BF16)
    r_lo = (r_pad - r_hi.astype(F32)).astype(BF16)
    routed, x_tiles = _router(x2, g, r_hi, r_lo)
    expert_of = routed[:, 0:2].astype(jnp.int32).T.reshape(-1)

    n_steps = (2 * t) // tm + N_EXPERTS
    n_pad = n_steps * tm - 2 * t
    counts = jnp.sum((expert_of[:, None] == jnp.arange(N_EXPERTS, dtype=jnp.int32)[None, :])
                     .astype(jnp.int32), axis=0)
    padded = ((counts + tm - 1) // tm) * tm
    pad_ends = jnp.cumsum(padded - counts)
    filler = jnp.arange(n_pad, dtype=jnp.int32)
    filler_expert = jnp.sum((filler[:, None] >= pad_ends[None, :]).astype(jnp.int32), axis=1)
    shift = 1 + (2 * t - 1).bit_length()
    keys = jnp.concatenate([expert_of * (1 << shift) + jnp.arange(2 * t, dtype=jnp.int32),
                            filler_expert * (1 << shift) + (1 << (shift - 1)) + filler])
    vals = jnp.concatenate([jnp.arange(2 * t, dtype=jnp.int32), 2 * t + tm + filler])
    _, dst = lax.sort((keys, vals), num_keys=1)
    src = jnp.where(dst < 2 * t, dst % t, 0)
    spare_tile = 2 * t + jnp.arange(tm, dtype=jnp.int32)
    dst_tiles = jnp.concatenate([spare_tile, dst]).reshape(n_steps + 1, 1, tm)
    src_tiles = jnp.concatenate([src, jnp.zeros((tm,), jnp.int32)]).reshape(n_steps + 1, 1, tm)
    ends = jnp.cumsum(padded)
    n_tiles = (ends[-1] // tm).astype(jnp.int32).reshape(1)
    tile_start = jnp.minimum(jnp.arange(n_steps, dtype=jnp.int32), n_tiles[0] - 1) * tm
    tile_expert = jnp.sum((tile_start[:, None] >= ends[None, :]).astype(jnp.int32), axis=1)

    y_tiles = _experts(x_tiles, src_tiles, dst_tiles, tile_expert, n_tiles, g, wg, wu, wd)
    return _combine(x2, y_tiles, routed)


def _block_diag(w):
    g, n, _ = w.shape
    eye = jnp.eye(g, dtype=w.dtype)
    return (eye[:, None, :, None] * w[:, :, None, :]).reshape(g * n, g * n)


def kernel(x, mem, positions, norm_mix_g, w_in, conv_w, conv_b, conv_ln_g, conv_ln_b, pool_w, pool_scale, w_gate, w_branch, w_out, norm_xa_g, norm_mem_g, xa_wq, xa_wkv, xa_q_norm_g, xa_k_norm_g, xa_wo, norm_ffn_g, ffn_w_gu, ffn_w_down, moe_router, moe_w_gu, moe_w_down):
    b, s, d = x.shape
    depth = w_in.shape[0]
    t = b * s
    assert d == D_MODEL and s % TOKEN_TILE == 0 and s % SEQ_TILE == 0

    def row(v):
        return v.astype(F32)[None, :]

    cos_t, sin_t = _rope_tables(positions)
    x2 = x.astype(F32).reshape(t, d)
    for l in range(depth):
        sb_q = slice(COL_SB_Q * D_BRANCH, (COL_SB_Q + 1) * D_BRANCH)
        w_in_l = w_in[l].astype(F32).at[:, sb_q].multiply(LOG2E * HEAD_DIM ** -0.5)
        u = _mix_in(x2, row(norm_mix_g[l]), w_in_l.astype(BF16))
        u3 = u.reshape(b, s, D_IN)
        cw = jnp.pad(conv_w[l].astype(F32), ((0, SEQ_HALO - CONV_WIDTH), (0, 0)))
        ycp = _conv_pool(u3, cw, row(conv_b[l]), row(conv_ln_g[l]), row(conv_ln_b[l]),
                         _block_diag(pool_w[l]).astype(BF16), row(pool_scale[l]))
        ysb = _stickbreak(u3)
        yret = _retention(u3, cos_t, sin_t)
        x2 = _merge(x2, ycp.reshape(t, 2 * D_BRANCH), ysb.reshape(t, D_BRANCH),
                    yret.reshape(t, D_BRANCH), row(norm_mix_g[l]), w_gate[l].astype(BF16),
                    w_branch[l].astype(BF16), w_out[l].astype(BF16))
        k_mem, v_mem = _mem_kv(mem.astype(F32), row(norm_mem_g[l]), xa_wkv[l].astype(BF16),
                               row(jnp.tile(xa_k_norm_g[l], N_HEADS)))
        x2 = _xattn(x2.reshape(b, s, d), k_mem, v_mem, row(norm_xa_g[l]), xa_wq[l].astype(BF16),
                    row(jnp.tile(xa_q_norm_g[l], N_HEADS)), xa_wo[l].astype(BF16)).reshape(t, d)
        g_ffn = row(norm_ffn_g[l])
        if l % 2 == 0:
            w_gu = ffn_w_gu[l // 2].astype(BF16)
            x2 = _ffn(x2, g_ffn, w_gu[:, :D_FF], w_gu[:, D_FF:], ffn_w_down[l // 2].astype(BF16))
        else:
            w_gu = moe_w_gu[l // 2].astype(BF16)
            x2 = _moe(x2, g_ffn, moe_router[l // 2].astype(F32), w_gu[:, :, :D_FF], w_gu[:, :, D_FF:],
                      moe_w_down[l // 2].astype(BF16))
    return x2.reshape(b, s, d).astype(x.dtype)
```

```python
import functools

import jax
import jax.numpy as jnp
from jax import lax
from jax.experimental import pallas as pl
from jax.experimental.pallas import tpu as pltpu

F32 = jnp.float32
BF16 = jnp.bfloat16

D_MODEL = 1024
HEAD_DIM = 64
N_HEADS = 4
D_BRANCH = N_HEADS * HEAD_DIM
CONV_WIDTH = 31
POOL_WINDOWS = (2, 4, 8, 16)
D_FF = 2816
N_EXPERTS = 8
ROPE_THETA = 10000.0
EPS = 1e-6

COL_CONV = 0
COL_POOL = 2
COL_SB_Q, COL_SB_K, COL_SB_V = 3, 4, 5
COL_RET_Q, COL_RET_K, COL_RET_V, COL_RET_G = 6, 7, 8, 9
D_IN = 10 * D_BRANCH

V7X_VMEM_BYTES = 64 * 1024 * 1024
VMEM_LIMIT_BYTES = V7X_VMEM_BYTES - 8 * 1024 * 1024

TOKEN_TILE = 512
SEQ_TILE = 256
SB_QUERY_TILE = 512
SB_KEY_TILE = 256
SB_CLAMP = 30.0
LOG2E = 1.4426950408889634
SEQ_HALO = 32
CONV_ROWS = 64
FF_CHUNK = 256
EXPERT_TILE = 512
ROUTER_LANES = 128


def _resident(shape):
    return pl.BlockSpec(shape, lambda *_: (0,) * len(shape), pipeline_mode=pl.Buffered(1))


def _params(*semantics):
    return pltpu.CompilerParams(dimension_semantics=semantics, vmem_limit_bytes=VMEM_LIMIT_BYTES)


def _sigmoid(x):
    return 0.5 * jnp.tanh(0.5 * x) + 0.5


def _silu(x):
    return x * _sigmoid(x)


def _rms_rows(x, g):
    ms = jnp.mean(x * x, axis=-1, keepdims=True)
    return x * lax.rsqrt(ms + EPS) * g


def _dot(a, b):
    return jnp.dot(a, b, preferred_element_type=F32)


def _dot_nt(a, b):
    return lax.dot_general(a, b, (((1,), (1,)), ((), ())), preferred_element_type=F32)


def _dot_tn(a, b):
    return lax.dot_general(a, b, (((0,), (0,)), ((), ())), preferred_element_type=F32)


def _split_dot(x, w):
    hi = x.astype(BF16)
    lo = (x - hi.astype(F32)).astype(BF16)
    return _dot(hi, w) + _dot(lo, w)


def _head_expand(x, lane_head):
    zero = jnp.zeros_like(x)
    return jnp.concatenate([jnp.where(lane_head == h, x, zero) for h in range(N_HEADS)], axis=0)


def _lane_head(width=D_BRANCH):
    return lax.shift_right_logical(lax.broadcasted_iota(jnp.int32, (1, width), 1), 6)


def _group_mean_matrix():
    r = lax.broadcasted_iota(jnp.int32, (D_BRANCH, D_BRANCH), 0)
    c = lax.broadcasted_iota(jnp.int32, (D_BRANCH, D_BRANCH), 1)
    same = lax.shift_right_logical(r, 6) == lax.shift_right_logical(c, 6)
    return jnp.where(same, 1.0 / HEAD_DIM, 0.0).astype(BF16)


def _mix_in_body(x_ref, g_ref, w_ref, u_ref):
    h = _rms_rows(x_ref[...], g_ref[...]).astype(BF16)
    u = _dot(h, w_ref[...])
    lo, hi = COL_SB_Q * D_BRANCH, (COL_SB_Q + 1) * D_BRANCH
    u_ref[:, :lo] = u[:, :lo].astype(BF16)
    u_ref[:, lo:hi] = (u[:, lo:hi] * (LOG2E * HEAD_DIM ** -0.5)).astype(BF16)
    u_ref[:, hi:] = u[:, hi:].astype(BF16)


def _mix_in(x2, g, w):
    t = x2.shape[0]
    return pl.pallas_call(
        _mix_in_body,
        grid=(t // TOKEN_TILE,),
        in_specs=[pl.BlockSpec((TOKEN_TILE, D_MODEL), lambda i: (i, 0)),
                  _resident((1, D_MODEL)), _resident((D_MODEL, D_IN))],
        out_specs=pl.BlockSpec((TOKEN_TILE, D_IN), lambda i: (i, 0)),
        out_shape=jax.ShapeDtypeStruct((t, D_IN), BF16),
        compiler_params=_params("parallel"),
        name="mix_in",
    )(x2, g, w)


def _conv_pool_body(uc_ref, uch_ref, up_ref, uph_ref, cw_ref, cb_ref, lg_ref, lb_ref,
                    pw_ref, ps_ref, y_ref, vs_ref, sh_ref, co_ref, pp_ref, q_ref):
    ts = uc_ref.shape[1]
    i = pl.program_id(1)
    first = i == 0

    uc = uc_ref[0].astype(F32)
    uh = uch_ref[0].astype(F32)
    v_halo = uh[:, :D_BRANCH] * _sigmoid(uh[:, D_BRANCH:])
    vs_ref[0:SEQ_HALO, :] = jnp.where(first, 0.0, v_halo)
    vs_ref[SEQ_HALO:SEQ_HALO + ts, :] = uc[:, :D_BRANCH] * _sigmoid(uc[:, D_BRANCH:])
    span = ts + SEQ_HALO - 8
    for r in range(1, 8):
        sh_ref[r, 0:span, :] = vs_ref[r:r + span, :]
    lead = SEQ_HALO - (CONV_WIDTH - 1)
    for r0 in range(0, ts, CONV_ROWS):
        acc = jnp.zeros((CONV_ROWS, D_BRANCH), F32)
        for k in range(CONV_WIDTH):
            a, r = divmod(lead + k, 8)
            rows = slice(r0 + 8 * a, r0 + 8 * a + CONV_ROWS)
            tap = vs_ref[rows, :] if r == 0 else sh_ref[r, rows, :]
            acc = acc + cw_ref[k:k + 1, :] * tap
        co_ref[r0:r0 + CONV_ROWS, :] = acc
    c = co_ref[...] + cb_ref[...]
    mu = jnp.mean(c, axis=-1, keepdims=True)
    cc = c - mu
    var = jnp.mean(cc * cc, axis=-1, keepdims=True)
    y_ref[0, :, 0:D_BRANCH] = _silu(cc * lax.rsqrt(var + EPS) * lg_ref[...] + lb_ref[...]).astype(BF16)

    up = up_ref[0].astype(F32)
    pp_ref[0:SEQ_HALO, :] = jnp.where(first, 0.0, uph_ref[0].astype(F32))
    pp_ref[SEQ_HALO:SEQ_HALO + ts, :] = up
    end = SEQ_HALO + ts
    t_pos = (i * ts + lax.broadcasted_iota(jnp.int32, (ts, 1), 0)).astype(F32) + 1.0
    lane = lax.broadcasted_iota(jnp.int32, (1, 128), 1)
    means = []
    for half in range(2):
        cols = slice(half * 128, (half + 1) * 128)
        q_ref[0, 8:end, cols] = pp_ref[8:end, cols] + pp_ref[7:end - 1, cols]
        q_ref[1, 16:end, cols] = q_ref[0, 16:end, cols] + q_ref[0, 14:end - 2, cols]
        if half == 0:
            small, big = q_ref[0, SEQ_HALO:end, cols], q_ref[1, SEQ_HALO:end, cols]
        else:
            q_ref[2, 24:end, cols] = q_ref[1, 24:end, cols] + q_ref[1, 20:end - 4, cols]
            small = q_ref[2, SEQ_HALO:end, cols]
            big = small + q_ref[2, SEQ_HALO - 8:end - 8, cols]
        w_small, w_big = POOL_WINDOWS[2 * half], POOL_WINDOWS[2 * half + 1]
        m_small = small / jnp.minimum(t_pos, float(w_small))
        m_big = big / jnp.minimum(t_pos, float(w_big))
        means.append(jnp.where(lane < HEAD_DIM, m_small, m_big))
    d = (jnp.concatenate(means, axis=1) - up).astype(BF16)
    y_ref[0, :, D_BRANCH:2 * D_BRANCH] = (_dot(d, pw_ref[...]) * ps_ref[...]).astype(BF16)


def _conv_pool(u3, conv_w, conv_b, ln_g, ln_b, pool_w_bd, pool_scale):
    b, s, _ = u3.shape
    ts = SEQ_TILE
    per = ts // SEQ_HALO

    def halo(col):
        return lambda bi, i: (bi, jnp.maximum(i * per - 1, 0), col)

    return pl.pallas_call(
        _conv_pool_body,
        grid=(b, s // ts),
        in_specs=[pl.BlockSpec((1, ts, 2 * D_BRANCH), lambda bi, i: (bi, i, COL_CONV)),
                  pl.BlockSpec((1, SEQ_HALO, 2 * D_BRANCH), halo(COL_CONV)),
                  pl.BlockSpec((1, ts, D_BRANCH), lambda bi, i: (bi, i, COL_POOL)),
                  pl.BlockSpec((1, SEQ_HALO, D_BRANCH), halo(COL_POOL)),
                  _resident((SEQ_HALO, D_BRANCH)), _resident((1, D_BRANCH)),
                  _resident((1, D_BRANCH)), _resident((1, D_BRANCH)),
                  _resident((D_BRANCH, D_BRANCH)), _resident((1, D_BRANCH))],
        out_specs=pl.BlockSpec((1, ts, 2 * D_BRANCH), lambda bi, i: (bi, i, 0)),
        out_shape=jax.ShapeDtypeStruct((b, s, 2 * D_BRANCH), BF16),
        scratch_shapes=[pltpu.VMEM((SEQ_HALO + ts, D_BRANCH), F32),
                        pltpu.VMEM((8, SEQ_HALO + ts, D_BRANCH), F32),
                        pltpu.VMEM((ts, D_BRANCH), F32),
                        pltpu.VMEM((SEQ_HALO + ts, D_BRANCH), F32),
                        pltpu.VMEM((3, SEQ_HALO + ts, D_BRANCH), F32)],
        compiler_params=_params("parallel", "parallel"),
        name="conv_pool",
    )(u3, u3, u3, u3, conv_w, conv_b, ln_g, ln_b, pool_w_bd, pool_scale)


def _stickbreak_body(q_ref, k_ref, v_ref, o_ref, kexp_ref, vexp_ref, w_ref, p_ref, f_ref, nr_ref,
                     acc_ref):
    tq = q_ref.shape[1]
    tk = SB_KEY_TILE
    n_kb = k_ref.shape[1] // tk
    per_q = tq // tk
    assert per_q % 2 == 0
    qi = pl.program_id(1)
    lane_head = _lane_head()

    @pl.when(qi == 0)
    def _():
        row_head = lax.shift_right_logical(lax.broadcasted_iota(jnp.int32, (D_BRANCH, 1), 0), 6)

        def expand(j, carry):
            start = pl.multiple_of(j * tk, tk)
            kt = k_ref[0, pl.ds(start, tk), :].astype(F32).T
            kexp_ref[j] = jnp.concatenate(
                [jnp.where(row_head == h, kt, 0.0) for h in range(N_HEADS)], axis=1).astype(BF16)
            vexp_ref[j] = _head_expand(v_ref[0, pl.ds(start, tk), :], lane_head)
            return carry

        lax.fori_loop(0, n_kb, expand, 0)

    q = q_ref[0]
    row = lax.broadcasted_iota(jnp.int32, (tk, tk), 0)
    col = lax.broadcasted_iota(jnp.int32, (tk, tk), 1)
    tri = (row >= col).astype(BF16)

    n_blocks = (qi + 1) * per_q

    def key_block(j):
        return jnp.maximum(n_blocks - 1 - j, 0)

    def logits(j, slot):
        w_ref[slot] = _dot(q, kexp_ref[key_block(j)])

    def probs(slot, mask):
        neg_run = nr_ref[...]
        f_ref[slot] = jnp.exp2(neg_run)
        total = None
        for h in range(N_HEADS):
            w = w_ref[slot, :, h * tk:(h + 1) * tk]
            sp = jnp.maximum(jnp.log2(1.0 + jnp.exp2(jnp.minimum(w, SB_CLAMP))), w)
            if mask is not None:
                sp = jnp.where(mask, sp, 0.0)
            rev = _dot(sp.astype(BF16), tri)
            p = jnp.exp2(w - rev)
            if mask is not None:
                p = jnp.where(mask, p, 0.0)
            p_ref[slot, :, h * tk:(h + 1) * tk] = p.astype(BF16)
            total = rev[:, 0:1] if total is None else jnp.where(lane_head == h, rev[:, 0:1], total)
        nr_ref[...] = neg_run - total

    def accumulate(j, slot):
        acc_ref[...] += _dot(p_ref[slot], vexp_ref[key_block(j)]) * f_ref[slot]

    acc_ref[...] = jnp.zeros_like(acc_ref)
    nr_ref[...] = jnp.zeros_like(nr_ref)
    t_loc = lax.broadcasted_iota(jnp.int32, (tq, tk), 0)
    s_loc = lax.broadcasted_iota(jnp.int32, (tq, tk), 1)
    logits(0, 0)
    for j in range(per_q):
        logits(j + 1, (j + 1) % 2)
        probs(j % 2, ((per_q - 1 - j) * tk + s_loc) < t_loc)
        if j > 0:
            accumulate(j - 1, (j - 1) % 2)

    def pair(it, carry):
        j = per_q + 2 * it
        logits(j + 1, 1)
        probs(0, None)
        accumulate(j - 1, 1)
        logits(j + 2, 0)
        probs(1, None)
        accumulate(j, 0)
        return carry

    lax.fori_loop(0, qi * (per_q // 2), pair, 0)
    accumulate(n_blocks - 1, 1)
    o_ref[0] = acc_ref[...].astype(BF16)


def _stickbreak(u3):
    b, s, _ = u3.shape
    tq, tk = SB_QUERY_TILE, SB_KEY_TILE
    return pl.pallas_call(
        _stickbreak_body,
        grid=(b, s // tq),
        in_specs=[pl.BlockSpec((1, tq, D_BRANCH), lambda bi, i: (bi, i, COL_SB_Q)),
                  pl.BlockSpec((1, s, D_BRANCH), lambda bi, i: (bi, 0, COL_SB_K)),
                  pl.BlockSpec((1, s, D_BRANCH), lambda bi, i: (bi, 0, COL_SB_V))],
        out_specs=pl.BlockSpec((1, tq, D_BRANCH), lambda bi, i: (bi, i, 0)),
        out_shape=jax.ShapeDtypeStruct((b, s, D_BRANCH), BF16),
        scratch_shapes=[pltpu.VMEM((s // tk, D_BRANCH, N_HEADS * tk), BF16),
                        pltpu.VMEM((s // tk, N_HEADS * tk, D_BRANCH), BF16),
                        pltpu.VMEM((2, tq, N_HEADS * tk), F32),
                        pltpu.VMEM((2, tq, N_HEADS * tk), BF16),
                        pltpu.VMEM((2, tq, D_BRANCH), F32),
                        pltpu.VMEM((tq, D_BRANCH), F32),
                        pltpu.VMEM((tq, D_BRANCH), F32)],
        compiler_params=_params("arbitrary", "arbitrary"),
        name="stickbreak",
    )(u3, u3, u3)


def _rope_body(pos_ref, f_ref, cos_ref, sin_ref):
    ang = pos_ref[0].astype(F32) * f_ref[...]
    cos_ref[0] = jnp.cos(ang)
    sin_ref[0] = jnp.sin(ang)


def _rope_tables(positions):
    b, s = positions.shape
    half = HEAD_DIM // 2
    inv_freq = ROPE_THETA ** (-jnp.arange(half, dtype=F32) / half)
    freq = jnp.tile(inv_freq, 128 // half)[None, :]
    ts = SEQ_TILE
    out = jax.ShapeDtypeStruct((b, s, 128), F32)
    return pl.pallas_call(
        _rope_body,
        grid=(b, s // ts),
        in_specs=[pl.BlockSpec((1, ts, 1), lambda bi, i: (bi, i, 0)), _resident((1, 128))],
        out_specs=[pl.BlockSpec((1, ts, 128), lambda bi, i: (bi, i, 0))] * 2,
        out_shape=[out, out],
        compiler_params=_params("parallel", "parallel"),
        name="rope_tables",
    )(positions[:, :, None], freq)


def _retention_body(q_ref, k_ref, v_ref, g_ref, cos_ref, sin_ref, dec_ref, xi_ref, zeta_ref,
                    gam_ref, o_ref, state_ref):
    @pl.when(pl.program_id(1) == 0)
    def _():
        state_ref[...] = jnp.zeros_like(state_ref)

    lane = lax.broadcasted_iota(jnp.int32, (1, D_BRANCH), 1)
    lane_head = lax.shift_right_logical(lane, 6)
    first_half = (lane & (HEAD_DIM - 1)) < HEAD_DIM // 2
    cos = jnp.concatenate([cos_ref[0], cos_ref[0]], axis=1)
    sin = jnp.concatenate([sin_ref[0], sin_ref[0]], axis=1)

    def rope(x):
        partner = jnp.where(first_half, -pltpu.roll(x, D_BRANCH - HEAD_DIM // 2, 1),
                            pltpu.roll(x, HEAD_DIM // 2, 1))
        return x * cos + partner * sin

    q = rope(q_ref[0].astype(F32))
    k = rope(k_ref[0].astype(F32)) * HEAD_DIM ** -0.5
    v = v_ref[0]
    kexp = _head_expand(k.astype(BF16), lane_head)
    vexp = _head_expand(v, lane_head)
    scores = _dot_nt(q.astype(BF16), kexp) * dec_ref[...]
    inner = _dot(scores.astype(BF16), vexp)
    state = state_ref[...]
    cross = _dot((q * xi_ref[...]).astype(BF16), state.astype(BF16))
    update = _dot_tn((k * zeta_ref[...]).astype(BF16), v)
    r = lax.broadcasted_iota(jnp.int32, (D_BRANCH, D_BRANCH), 0)
    c = lax.broadcasted_iota(jnp.int32, (D_BRANCH, D_BRANCH), 1)
    same_head = lax.shift_right_logical(r, 6) == lax.shift_right_logical(c, 6)
    state_ref[...] = state * gam_ref[...] + jnp.where(same_head, update, 0.0)

    o = inner + cross
    gm = _group_mean_matrix()
    oc = o - _split_dot(o, gm)
    var = _split_dot(oc * oc, gm)
    o_ref[0] = (_silu(g_ref[0].astype(F32)) * (oc * lax.rsqrt(var + EPS))).astype(BF16)


def _retention(u3, cos_t, sin_t):
    b, s, _ = u3.shape
    tc = SEQ_TILE
    gammas = 1.0 - jnp.exp2(-5.0 - jnp.arange(N_HEADS, dtype=F32))
    log_g = jnp.log(gammas)
    log_g_lane = jnp.repeat(log_g, HEAD_DIM)[None, :]
    idx = jnp.arange(tc, dtype=F32)
    rel = idx[:, None] - idx[None, :]
    decay = jnp.where(rel >= 0, jnp.exp(jnp.maximum(rel, 0.0) * log_g[:, None, None]), 0.0)
    decay = jnp.transpose(decay, (1, 0, 2)).reshape(tc, N_HEADS * tc)
    xi = jnp.exp((idx + 1.0)[:, None] * log_g_lane)
    zeta = jnp.exp((tc - 1.0 - idx)[:, None] * log_g_lane)
    gam = jnp.broadcast_to(jnp.exp(tc * log_g_lane).T, (D_BRANCH, D_BRANCH))

    def col(cb):
        return pl.BlockSpec((1, tc, D_BRANCH), lambda bi, i: (bi, i, cb))

    tab = pl.BlockSpec((1, tc, 128), lambda bi, i: (bi, i, 0))
    return pl.pallas_call(
        _retention_body,
        grid=(b, s // tc),
        in_specs=[col(COL_RET_Q), col(COL_RET_K), col(COL_RET_V), col(COL_RET_G), tab, tab,
                  _resident((tc, N_HEADS * tc)), _resident((tc, D_BRANCH)),
                  _resident((tc, D_BRANCH)), _resident((D_BRANCH, D_BRANCH))],
        out_specs=pl.BlockSpec((1, tc, D_BRANCH), lambda bi, i: (bi, i, 0)),
        out_shape=jax.ShapeDtypeStruct((b, s, D_BRANCH), BF16),
        scratch_shapes=[pltpu.VMEM((D_BRANCH, D_BRANCH), F32)],
        compiler_params=_params("parallel", "arbitrary"),
        name="retention",
    )(u3, u3, u3, u3, cos_t, sin_t, decay, xi, zeta, gam)


def _merge_body(x_ref, ycp_ref, ysb_ref, yret_ref, g_ref, wg_ref, wb_ref, wo_ref, o_ref):
    x = x_ref[...]
    h = _rms_rows(x, g_ref[...]).astype(BF16)
    ycp = ycp_ref[...]
    branches = (ycp[:, :D_BRANCH], ycp[:, D_BRANCH:], ysb_ref[...], yret_ref[...])
    parts = []
    for n in range(D_MODEL // D_BRANCH):
        cols = slice(n * D_BRANCH, (n + 1) * D_BRANCH)
        m = None
        for i, y in enumerate(branches):
            term = _sigmoid(_dot(h, wg_ref[i, :, cols])) * _dot(y, wb_ref[i, :, cols])
            m = term if m is None else m + term
        parts.append(m.astype(BF16))
    o_ref[...] = x + _dot(jnp.concatenate(parts, axis=1), wo_ref[...])


def _merge(x2, ycp, ysb, yret, g, wg, wb, wo):
    t = x2.shape[0]
    tm = TOKEN_TILE

    def rows(width):
        return pl.BlockSpec((tm, width), lambda i: (i, 0))

    return pl.pallas_call(
        _merge_body,
        grid=(t // tm,),
        in_specs=[rows(D_MODEL), rows(2 * D_BRANCH), rows(D_BRANCH), rows(D_BRANCH),
                  _resident((1, D_MODEL)), _resident((N_HEADS, D_MODEL, D_MODEL)),
                  _resident((N_HEADS, D_BRANCH, D_MODEL)), _resident((D_MODEL, D_MODEL))],
        out_specs=rows(D_MODEL),
        out_shape=jax.ShapeDtypeStruct((t, D_MODEL), F32),
        compiler_params=_params("parallel"),
        name="merge",
    )(x2, ycp, ysb, yret, g, wg, wb, wo)


def _head_rms(x, gain_lanes, gm):
    ms = _split_dot(x * x, gm)
    return x * lax.rsqrt(ms + EPS) * gain_lanes


def _mem_kv_body(m_ref, g_ref, w_ref, gk_ref, k_ref, v_ref):
    hm = _rms_rows(m_ref[0], g_ref[...]).astype(BF16)
    kv = _dot(hm, w_ref[...])
    k_ref[0] = _head_rms(kv[:, :D_BRANCH], gk_ref[...], _group_mean_matrix()).astype(BF16)
    v_ref[0] = kv[:, D_BRANCH:].astype(BF16)


def _mem_kv(mem, g, wkv, gk_lanes):
    b, m, _ = mem.shape
    out = jax.ShapeDtypeStruct((b, m, D_BRANCH), BF16)
    blk = pl.BlockSpec((1, m, D_BRANCH), lambda bi: (bi, 0, 0))
    return pl.pallas_call(
        _mem_kv_body,
        grid=(b,),
        in_specs=[pl.BlockSpec((1, m, D_MODEL), lambda bi: (bi, 0, 0)), _resident((1, D_MODEL)),
                  _resident((D_MODEL, 2 * D_BRANCH)), _resident((1, D_BRANCH))],
        out_specs=[blk, blk],
        out_shape=[out, out],
        compiler_params=_params("parallel"),
        name="mem_kv",
    )(mem, g, wkv, gk_lanes)


def _xattn_body(x_ref, k_ref, v_ref, g_ref, wq_ref, gq_ref, wo_ref, o_ref):
    x = x_ref[0]
    m = k_ref.shape[1]
    h = _rms_rows(x, g_ref[...]).astype(BF16)
    q = _head_rms(_dot(h, wq_ref[...]), gq_ref[...], _group_mean_matrix()).astype(BF16)
    lane_head = _lane_head()
    s_all = _dot_nt(q, _head_expand(k_ref[0], lane_head)) * HEAD_DIM ** -0.5
    probs = []
    for hd in range(N_HEADS):
        s = s_all[:, hd * m:(hd + 1) * m]
        e = jnp.exp(s - jnp.max(s, axis=-1, keepdims=True))
        probs.append((e / jnp.sum(e, axis=-1, keepdims=True)).astype(BF16))
    o = _dot(jnp.concatenate(probs, axis=1), _head_expand(v_ref[0], lane_head))
    o_ref[0] = x + _dot(o.astype(BF16), wo_ref[...])


def _xattn(x3, k, v, g, wq, gq_lanes, wo):
    b, s, _ = x3.shape
    m = k.shape[1]
    ts = TOKEN_TILE
    xblk = pl.BlockSpec((1, ts, D_MODEL), lambda bi, i: (bi, i, 0))
    kvblk = pl.BlockSpec((1, m, D_BRANCH), lambda bi, i: (bi, 0, 0))
    return pl.pallas_call(
        _xattn_body,
        grid=(b, s // ts),
        in_specs=[xblk, kvblk, kvblk, _resident((1, D_MODEL)), _resident((D_MODEL, D_BRANCH)),
                  _resident((1, D_BRANCH)), _resident((D_BRANCH, D_MODEL))],
        out_specs=xblk,
        out_shape=jax.ShapeDtypeStruct((b, s, D_MODEL), F32),
        compiler_params=_params("parallel", "parallel"),
        name="xattn",
    )(x3, k, v, g, wq, gq_lanes, wo)


def _swiglu_rows(h, wgu_ref, wd_ref, acc_ref, between=None):
    for j in range(D_FF // FF_CHUNK):
        cols = slice(j * FF_CHUNK, (j + 1) * FF_CHUNK)
        up_cols = slice(D_FF + j * FF_CHUNK, D_FF + (j + 1) * FF_CHUNK)
        a = (_silu(_dot(h, wgu_ref[:, cols])) * _dot(h, wgu_ref[:, up_cols])).astype(BF16)
        part = _dot(a, wd_ref[cols, :])
        if j == 0:
            acc_ref[...] = part
        else:
            acc_ref[...] += part
        if between is not None:
            between(j)


def _ffn_body(x_ref, g_ref, wgu_ref, wd_ref, o_ref, acc_ref):
    x = x_ref[...]
    h = _rms_rows(x, g_ref[...]).astype(BF16)
    _swiglu_rows(h, wgu_ref, wd_ref, acc_ref)
    o_ref[...] = x + acc_ref[...]


def _ffn(x2, g, wgu, wd):
    t = x2.shape[0]
    tm = TOKEN_TILE
    rows = pl.BlockSpec((tm, D_MODEL), lambda i: (i, 0))
    return pl.pallas_call(
        _ffn_body,
        grid=(t // tm,),
        in_specs=[rows, _resident((1, D_MODEL)), _resident((D_MODEL, 2 * D_FF)),
                  _resident((D_FF, D_MODEL))],
        out_specs=rows,
        out_shape=jax.ShapeDtypeStruct((t, D_MODEL), F32),
        scratch_shapes=[pltpu.VMEM((tm, D_MODEL), F32)],
        compiler_params=_params("parallel"),
        name="ffn",
    )(x2, g, wgu, wd)


LANE_TILES = D_MODEL // 128


def _to_token_tiles(x, o_ref):
    n = x.shape[0]
    for c in range(LANE_TILES):
        o_ref[pl.ds(c, n, stride=LANE_TILES), :] = x[:, c * 128:(c + 1) * 128]


def _from_token_tiles(ref, n):
    return jnp.concatenate([ref[pl.ds(c, n, stride=LANE_TILES), :] for c in range(LANE_TILES)], axis=1)


def _router_body(x_ref, g_ref, rhi_ref, rlo_ref, o_ref, xt_ref):
    x = x_ref[...]
    _to_token_tiles(x, xt_ref)
    h = _rms_rows(x, g_ref[...])
    hi = h.astype(BF16)
    lo = (h - hi.astype(F32)).astype(BF16)
    logits = _dot(hi, rhi_ref[...]) + (_dot(hi, rlo_ref[...]) + _dot(lo, rhi_ref[...]))
    lane = lax.broadcasted_iota(jnp.int32, logits.shape, 1)
    neg = jnp.float32(-jnp.inf)
    logits = jnp.where(lane < N_EXPERTS, logits, neg)
    m1 = jnp.max(logits, axis=-1, keepdims=True)
    i1 = jnp.min(jnp.where(logits == m1, lane, ROUTER_LANES), axis=-1, keepdims=True)
    rest = jnp.where(lane == i1, neg, logits)
    m2 = jnp.max(rest, axis=-1, keepdims=True)
    i2 = jnp.min(jnp.where(rest == m2, lane, ROUTER_LANES), axis=-1, keepdims=True)
    e2 = jnp.exp(m2 - m1)
    w1 = 1.0 / (1.0 + e2)
    w2 = e2 / (1.0 + e2)
    out = jnp.where(lane == 0, i1.astype(F32), 0.0)
    out = jnp.where(lane == 1, i2.astype(F32), out)
    out = jnp.where(lane == 2, w1, out)
    o_ref[...] = jnp.where(lane == 3, w2, out)


def _router(x2, g, r_hi, r_lo):
    t = x2.shape[0]
    tm = TOKEN_TILE
    return pl.pallas_call(
        _router_body,
        grid=(t // tm,),
        in_specs=[pl.BlockSpec((tm, D_MODEL), lambda i: (i, 0)), _resident((1, D_MODEL)),
                  _resident((D_MODEL, ROUTER_LANES)), _resident((D_MODEL, ROUTER_LANES))],
        out_specs=[pl.BlockSpec((tm, ROUTER_LANES), lambda i: (i, 0)),
                   pl.BlockSpec((tm * LANE_TILES, 128), lambda i: (i, 0))],
        out_shape=[jax.ShapeDtypeStruct((t, ROUTER_LANES), F32),
                   jax.ShapeDtypeStruct((t * LANE_TILES, 128), F32)],
        compiler_params=_params("parallel"),
        name="router",
    )(x2, g, r_hi, r_lo)


def _experts_body(tile_expert_ref, n_tiles_ref, src_ref, src_next_ref, dst_prev_ref, dst_ref,
                  x_hbm, g_ref, wgu_ref, wd_ref, y_hbm, xbuf_ref, ybuf_ref, acc_ref, gsem, ssem):
    i = pl.program_id(0)
    last = pl.num_programs(0) - 1
    slot = lax.rem(i, 2)
    other = 1 - slot
    tm = EXPERT_TILE
    rows = LANE_TILES

    def gather(idx_ref, r, s):
        src = pl.multiple_of(idx_ref[0, 0, r] * rows, rows)
        return pltpu.make_async_copy(x_hbm.at[pl.ds(src, rows)],
                                     xbuf_ref.at[s, pl.ds(r * rows, rows)], gsem)

    def scatter(idx_ref, r, s):
        dst = pl.multiple_of(idx_ref[0, 0, r] * rows, rows)
        return pltpu.make_async_copy(ybuf_ref.at[s, pl.ds(r * rows, rows)],
                                     y_hbm.at[pl.ds(dst, rows)], ssem)

    def wait_gather():
        pltpu.make_async_copy(x_hbm.at[pl.ds(0, tm * rows)], xbuf_ref.at[0], gsem).wait()

    def wait_scatter():
        pltpu.make_async_copy(ybuf_ref.at[0], y_hbm.at[pl.ds(0, tm * rows)], ssem).wait()

    def start_next(r):
        gather(src_next_ref, r, other).start(priority=0)
        scatter(dst_prev_ref, r, other).start(priority=1)

    @pl.when(i == 0)
    def _():
        ybuf_ref[...] = jnp.zeros_like(ybuf_ref)
        lax.fori_loop(0, tm, lambda r, c: (gather(src_ref, r, 0).start(), c)[1], 0)

    wait_gather()

    @pl.when(i > 0)
    def _():
        wait_scatter()

    @pl.when(i < n_tiles_ref[0])
    def _():
        h = _rms_rows(_from_token_tiles(xbuf_ref.at[slot], tm), g_ref[...]).astype(BF16)
        n_chunks = D_FF // FF_CHUNK
        per = -(-tm // n_chunks)

        def between(j):
            for r in range(j * per, min((j + 1) * per, tm)):
                start_next(r)

        _swiglu_rows(h, wgu_ref.at[0], wd_ref.at[0], acc_ref, between)
        _to_token_tiles(acc_ref[...], ybuf_ref.at[slot])

    @pl.when(i >= n_tiles_ref[0])
    def _():
        ybuf_ref[slot] = jnp.zeros((tm * rows, 128), F32)
        lax.fori_loop(0, tm, lambda r, c: (start_next(r), c)[1], 0)

    @pl.when(i == last)
    def _():
        wait_gather()
        wait_scatter()
        lax.fori_loop(0, tm, lambda r, c: (scatter(dst_ref, r, slot).start(), c)[1], 0)
        wait_scatter()


def _experts(x_tiles, src_tiles, dst_tiles, tile_expert, n_tiles, g, wgu, wd):
    n_steps = src_tiles.shape[0] - 1
    tm = EXPERT_TILE

    def idx(off):
        return pl.BlockSpec((1, 1, tm), lambda i, te, nt: (i + off, 0, 0), memory_space=pltpu.SMEM)

    def expert(shape):
        return pl.BlockSpec((1,) + shape, lambda i, te, nt: (te[i], 0, 0),
                            pipeline_mode=pl.Buffered(1))

    grid_spec = pltpu.PrefetchScalarGridSpec(
        num_scalar_prefetch=2,
        grid=(n_steps,),
        in_specs=[idx(0), idx(1), idx(0), idx(1),
                  pl.BlockSpec(memory_space=pl.ANY),
                  pl.BlockSpec((1, D_MODEL), lambda i, te, nt: (0, 0)),
                  expert((D_MODEL, 2 * D_FF)), expert((D_FF, D_MODEL))],
        out_specs=pl.BlockSpec(memory_space=pl.ANY),
        scratch_shapes=[pltpu.VMEM((2, tm * LANE_TILES, 128), F32),
                        pltpu.VMEM((2, tm * LANE_TILES, 128), F32),
                        pltpu.VMEM((tm, D_MODEL), F32),
                        pltpu.SemaphoreType.DMA(()), pltpu.SemaphoreType.DMA(())],
    )
    return pl.pallas_call(
        _experts_body,
        grid_spec=grid_spec,
        out_shape=jax.ShapeDtypeStruct(((n_steps + 1) * tm * LANE_TILES, 128), F32),
        compiler_params=_params("arbitrary"),
        name="experts",
    )(tile_expert, n_tiles, src_tiles, src_tiles, dst_tiles, dst_tiles, x_tiles, g, wgu, wd)


def _combine_body(x_ref, y0_ref, y1_ref, r_ref, o_ref):
    routed = r_ref[...]
    tm = x_ref.shape[0]
    y0 = _from_token_tiles(y0_ref, tm)
    y1 = _from_token_tiles(y1_ref, tm)
    o_ref[...] = x_ref[...] + (routed[:, 2:3] * y0 + routed[:, 3:4] * y1)


def _combine(x2, y_tiles, routed):
    t = x2.shape[0]
    tm = TOKEN_TILE
    rows = pl.BlockSpec((tm, D_MODEL), lambda i: (i, 0))
    return pl.pallas_call(
        _combine_body,
        grid=(t // tm,),
        in_specs=[rows, pl.BlockSpec((tm * LANE_TILES, 128), lambda i: (i, 0)),
                  pl.BlockSpec((tm * LANE_TILES, 128), lambda i: (i + t // tm, 0)),
                  pl.BlockSpec((tm, ROUTER_LANES), lambda i: (i, 0))],
        out_specs=rows,
        out_shape=jax.ShapeDtypeStruct((t, D_MODEL), F32),
        compiler_params=_params("parallel"),
        name="combine",
    )(x2, y_tiles, y_tiles, routed)


def _moe(x2, g, router, wgu, wd):
    t = x2.shape[0]
    tm = EXPERT_TILE
    r_pad = jnp.pad(router, ((0, 0), (0, ROUTER_LANES - N_EXPERTS)))
    r_hi = r_pad.astype(BF16)
    r_lo = (r_pad - r_hi.astype(F32)).astype(BF16)
    routed, x_tiles = _router(x2, g, r_hi, r_lo)
    expert_of = routed[:, 0:2].astype(jnp.int32).T.reshape(-1)

    n_steps = (2 * t) // tm + N_EXPERTS
    n_pad = n_steps * tm - 2 * t
    counts = jnp.sum((expert_of[:, None] == jnp.arange(N_EXPERTS, dtype=jnp.int32)[None, :])
                     .astype(jnp.int32), axis=0)
    padded = ((counts + tm - 1) // tm) * tm
    pad_ends = jnp.cumsum(padded - counts)
    filler = jnp.arange(n_pad, dtype=jnp.int32)
    filler_expert = jnp.sum((filler[:, None] >= pad_ends[None, :]).astype(jnp.int32), axis=1)
    shift = 1 + (2 * t - 1).bit_length()
    keys = jnp.concatenate([expert_of * (1 << shift) + jnp.arange(2 * t, dtype=jnp.int32),
                            filler_expert * (1 << shift) + (1 << (shift - 1)) + filler])
    vals = jnp.concatenate([jnp.arange(2 * t, dtype=jnp.int32), 2 * t + tm + filler])
    _, dst = lax.sort((keys, vals), num_keys=1)
    src = jnp.where(dst < 2 * t, dst % t, 0)
    spare_tile = 2 * t + jnp.arange(tm, dtype=jnp.int32)
    dst_tiles = jnp.concatenate([spare_tile, dst]).reshape(n_steps + 1, 1, tm)
    src_tiles = jnp.concatenate([src, jnp.zeros((tm,), jnp.int32)]).reshape(n_steps + 1, 1, tm)
    ends = jnp.cumsum(padded)
    n_tiles = (ends[-1] // tm).astype(jnp.int32).reshape(1)
    tile_start = jnp.minimum(jnp.arange(n_steps, dtype=jnp.int32), n_tiles[0] - 1) * tm
    tile_expert = jnp.sum((tile_start[:, None] >= ends[None, :]).astype(jnp.int32), axis=1)

    y_tiles = _experts(x_tiles, src_tiles, dst_tiles, tile_expert, n_tiles, g, wgu, wd)
    return _combine(x2, y_tiles, routed)


def _block_diag(w):
    g, n, _ = w.shape
    eye = jnp.eye(g, dtype=w.dtype)
    return (eye[:, None, :, None] * w[:, :, None, :]).reshape(g * n, g * n)


def kernel(x, mem, positions, norm_mix_g, w_in, conv_w, conv_b, conv_ln_g, conv_ln_b, pool_w, pool_scale, w_gate, w_branch, w_out, norm_xa_g, norm_mem_g, xa_wq, xa_wkv, xa_q_norm_g, xa_k_norm_g, xa_wo, norm_ffn_g, ffn_w_gu, ffn_w_down, moe_router, moe_w_gu, moe_w_down):
    b, s, d = x.shape
    depth = w_in.shape[0]
    t = b * s
    assert d == D_MODEL and s % TOKEN_TILE == 0 and s % SEQ_TILE == 0

    def row(v):
        return v.astype(F32)[None, :]

    cos_t, sin_t = _rope_tables(positions)
    x2 = x.astype(F32).reshape(t, d)
    for l in range(depth):
        u = _mix_in(x2, row(norm_mix_g[l]), w_in[l].astype(BF16))
        u3 = u.reshape(b, s, D_IN)
        cw = jnp.pad(conv_w[l].astype(F32), ((0, SEQ_HALO - CONV_WIDTH), (0, 0)))
        ycp = _conv_pool(u3, cw, row(conv_b[l]), row(conv_ln_g[l]), row(conv_ln_b[l]),
                         _block_diag(pool_w[l]).astype(BF16), row(pool_scale[l]))
        ysb = _stickbreak(u3)
        yret = _retention(u3, cos_t, sin_t)
        x2 = _merge(x2, ycp.reshape(t, 2 * D_BRANCH), ysb.reshape(t, D_BRANCH),
                    yret.reshape(t, D_BRANCH), row(norm_mix_g[l]), w_gate[l].astype(BF16),
                    w_branch[l].astype(BF16), w_out[l].astype(BF16))
        k_mem, v_mem = _mem_kv(mem.astype(F32), row(norm_mem_g[l]), xa_wkv[l].astype(BF16),
                               row(jnp.tile(xa_k_norm_g[l], N_HEADS)))
        x2 = _xattn(x2.reshape(b, s, d), k_mem, v_mem, row(norm_xa_g[l]), xa_wq[l].astype(BF16),
                    row(jnp.tile(xa_q_norm_g[l], N_HEADS)), xa_wo[l].astype(BF16)).reshape(t, d)
        g_ffn = row(norm_ffn_g[l])
        if l % 2 == 0:
            x2 = _ffn(x2, g_ffn, ffn_w_gu[l // 2].astype(BF16), ffn_w_down[l // 2].astype(BF16))
        else:
            x2 = _moe(x2, g_ffn, moe_router[l // 2].astype(F32), moe_w_gu[l // 2].astype(BF16),
                      moe_w_down[l // 2].astype(BF16))
    return x2.reshape(b, s, d).astype(x.dtype)
```

```python
import functools

import jax
import jax.numpy as jnp
from jax import lax
from jax.experimental import pallas as pl
from jax.experimental.pallas import tpu as pltpu

F32 = jnp.float32
BF16 = jnp.bfloat16

D_MODEL = 1024
HEAD_DIM = 64
N_HEADS = 4
D_BRANCH = N_HEADS * HEAD_DIM
CONV_WIDTH = 31
POOL_WINDOWS = (2, 4, 8, 16)
D_FF = 2816
N_EXPERTS = 8
ROPE_THETA = 10000.0
EPS = 1e-6

COL_CONV = 0
COL_POOL = 2
D_IN = 10 * D_BRANCH
COL_SB_Q, COL_SB_K, COL_SB_V = 0, 1, 2
COL_RET_Q, COL_RET_K, COL_RET_V, COL_RET_G = 3, 4, 5, 6
D_ATT = 7 * D_BRANCH

V7X_VMEM_BYTES = 64 * 1024 * 1024
VMEM_LIMIT_BYTES = V7X_VMEM_BYTES - 8 * 1024 * 1024

TOKEN_TILE = 512
SEQ_TILE = 256
SB_QUERY_TILE = 512
SB_KEY_TILE = 256
SB_CLAMP = 30.0
LOG2E = 1.4426950408889634
SEQ_HALO = 32
CONV_ROWS = 64
FF_CHUNK = 256
EXPERT_TILE = 512
ROUTER_LANES = 128


def _resident(shape):
    return pl.BlockSpec(shape, lambda *_: (0,) * len(shape), pipeline_mode=pl.Buffered(1))


def _params(*semantics):
    return pltpu.CompilerParams(dimension_semantics=semantics, vmem_limit_bytes=VMEM_LIMIT_BYTES)


def _sigmoid(x):
    return 0.5 * jnp.tanh(0.5 * x) + 0.5


def _silu(x):
    return x * _sigmoid(x)


def _rms_rows(x, g):
    ms = jnp.mean(x * x, axis=-1, keepdims=True)
    return x * lax.rsqrt(ms + EPS) * g


def _dot(a, b):
    return jnp.dot(a, b, preferred_element_type=F32)


def _dot_nt(a, b):
    return lax.dot_general(a, b, (((1,), (1,)), ((), ())), preferred_element_type=F32)


def _dot_tn(a, b):
    return lax.dot_general(a, b, (((0,), (0,)), ((), ())), preferred_element_type=F32)


def _split_dot(x, w):
    hi = x.astype(BF16)
    lo = (x - hi.astype(F32)).astype(BF16)
    return _dot(hi, w) + _dot(lo, w)


def _head_expand(x, lane_head):
    zero = jnp.zeros_like(x)
    return jnp.concatenate([jnp.where(lane_head == h, x, zero) for h in range(N_HEADS)], axis=0)


def _lane_head(width=D_BRANCH):
    return lax.shift_right_logical(lax.broadcasted_iota(jnp.int32, (1, width), 1), 6)


def _group_mean_matrix():
    r = lax.broadcasted_iota(jnp.int32, (D_BRANCH, D_BRANCH), 0)
    c = lax.broadcasted_iota(jnp.int32, (D_BRANCH, D_BRANCH), 1)
    same = lax.shift_right_logical(r, 6) == lax.shift_right_logical(c, 6)
    return jnp.where(same, 1.0 / HEAD_DIM, 0.0).astype(BF16)


def _mix_front_body(x_ref, g_ref, w_ref, cw_ref, cb_ref, lg_ref, lb_ref, pw_ref, ps_ref,
                    y_ref, ua_ref, vs_ref, sh_ref, co_ref, pp_ref, q_ref):
    ts = x_ref.shape[1]
    i = pl.program_id(1)
    h = _rms_rows(x_ref[0], g_ref[...]).astype(BF16)
    att = (COL_POOL + 1) * D_BRANCH
    u = _dot(h, w_ref[:, 0:att])
    ua = _dot(h, w_ref[:, att:])
    ua_ref[0, :, 0:D_BRANCH] = (ua[:, 0:D_BRANCH] * (LOG2E * HEAD_DIM ** -0.5)).astype(BF16)
    ua_ref[0, :, D_BRANCH:] = ua[:, D_BRANCH:].astype(BF16)

    @pl.when(i == 0)
    def _():
        vs_ref[0:SEQ_HALO, :] = jnp.zeros((SEQ_HALO, D_BRANCH), F32)
        pp_ref[0:SEQ_HALO, :] = jnp.zeros((SEQ_HALO, D_BRANCH), F32)

    vs_ref[SEQ_HALO:SEQ_HALO + ts, :] = u[:, 0:D_BRANCH] * _sigmoid(u[:, D_BRANCH:2 * D_BRANCH])
    span = ts + SEQ_HALO - 8
    for r in range(1, 8):
        sh_ref[r, 0:span, :] = vs_ref[r:r + span, :]
    lead = SEQ_HALO - (CONV_WIDTH - 1)
    for r0 in range(0, ts, CONV_ROWS):
        acc = jnp.zeros((CONV_ROWS, D_BRANCH), F32)
        for k in range(CONV_WIDTH):
            a, r = divmod(lead + k, 8)
            rows = slice(r0 + 8 * a, r0 + 8 * a + CONV_ROWS)
            tap = vs_ref[rows, :] if r == 0 else sh_ref[r, rows, :]
            acc = acc + cw_ref[k:k + 1, :] * tap
        co_ref[r0:r0 + CONV_ROWS, :] = acc
    c = co_ref[...] + cb_ref[...]
    mu = jnp.mean(c, axis=-1, keepdims=True)
    cc = c - mu
    var = jnp.mean(cc * cc, axis=-1, keepdims=True)
    y_ref[0, :, 0:D_BRANCH] = _silu(cc * lax.rsqrt(var + EPS) * lg_ref[...] + lb_ref[...]).astype(BF16)
    vs_ref[0:SEQ_HALO, :] = vs_ref[ts:ts + SEQ_HALO, :]

    up = u[:, COL_POOL * D_BRANCH:att]
    pp_ref[SEQ_HALO:SEQ_HALO + ts, :] = up
    end = SEQ_HALO + ts
    t_pos = (i * ts + lax.broadcasted_iota(jnp.int32, (ts, 1), 0)).astype(F32) + 1.0
    lane = lax.broadcasted_iota(jnp.int32, (1, 128), 1)
    means = []
    for half in range(2):
        cols = slice(half * 128, (half + 1) * 128)
        q_ref[0, 8:end, cols] = pp_ref[8:end, cols] + pp_ref[7:end - 1, cols]
        q_ref[1, 16:end, cols] = q_ref[0, 16:end, cols] + q_ref[0, 14:end - 2, cols]
        if half == 0:
            small, big = q_ref[0, SEQ_HALO:end, cols], q_ref[1, SEQ_HALO:end, cols]
        else:
            q_ref[2, 24:end, cols] = q_ref[1, 24:end, cols] + q_ref[1, 20:end - 4, cols]
            small = q_ref[2, SEQ_HALO:end, cols]
            big = small + q_ref[2, SEQ_HALO - 8:end - 8, cols]
        w_small, w_big = POOL_WINDOWS[2 * half], POOL_WINDOWS[2 * half + 1]
        m_small = small / jnp.minimum(t_pos, float(w_small))
        m_big = big / jnp.minimum(t_pos, float(w_big))
        means.append(jnp.where(lane < HEAD_DIM, m_small, m_big))
    d = (jnp.concatenate(means, axis=1) - up).astype(BF16)
    y_ref[0, :, D_BRANCH:2 * D_BRANCH] = (_dot(d, pw_ref[...]) * ps_ref[...]).astype(BF16)
    pp_ref[0:SEQ_HALO, :] = pp_ref[ts:ts + SEQ_HALO, :]


def _mix_front(x3, g, w, conv_w, conv_b, ln_g, ln_b, pool_w_bd, pool_scale):
    b, s, _ = x3.shape
    ts = TOKEN_TILE
    return pl.pallas_call(
        _mix_front_body,
        grid=(b, s // ts),
        in_specs=[pl.BlockSpec((1, ts, D_MODEL), lambda bi, i: (bi, i, 0)),
                  _resident((1, D_MODEL)), _resident((D_MODEL, D_IN)),
                  _resident((SEQ_HALO, D_BRANCH)), _resident((1, D_BRANCH)),
                  _resident((1, D_BRANCH)), _resident((1, D_BRANCH)),
                  _resident((D_BRANCH, D_BRANCH)), _resident((1, D_BRANCH))],
        out_specs=[pl.BlockSpec((1, ts, 2 * D_BRANCH), lambda bi, i: (bi, i, 0)),
                   pl.BlockSpec((1, ts, D_ATT), lambda bi, i: (bi, i, 0))],
        out_shape=[jax.ShapeDtypeStruct((b, s, 2 * D_BRANCH), BF16),
                   jax.ShapeDtypeStruct((b, s, D_ATT), BF16)],
        scratch_shapes=[pltpu.VMEM((SEQ_HALO + ts, D_BRANCH), F32),
                        pltpu.VMEM((8, SEQ_HALO + ts, D_BRANCH), F32),
                        pltpu.VMEM((ts, D_BRANCH), F32),
                        pltpu.VMEM((SEQ_HALO + ts, D_BRANCH), F32),
                        pltpu.VMEM((3, SEQ_HALO + ts, D_BRANCH), F32)],
        compiler_params=_params("arbitrary", "arbitrary"),
        name="mix_front",
    )(x3, g, w, conv_w, conv_b, ln_g, ln_b, pool_w_bd, pool_scale)


def _stickbreak_body(q_ref, k_ref, v_ref, o_ref, kexp_ref, vexp_ref, w_ref, p_ref, f_ref, nr_ref,
                     acc_ref):
    tq = q_ref.shape[1]
    tk = SB_KEY_TILE
    n_kb = k_ref.shape[1] // tk
    per_q = tq // tk
    assert per_q % 2 == 0
    qi = pl.program_id(1)
    lane_head = _lane_head()

    @pl.when(qi == 0)
    def _():
        row_head = lax.shift_right_logical(lax.broadcasted_iota(jnp.int32, (D_BRANCH, 1), 0), 6)

        def expand(j, carry):
            start = pl.multiple_of(j * tk, tk)
            kt = k_ref[0, pl.ds(start, tk), :].astype(F32).T
            kexp_ref[j] = jnp.concatenate(
                [jnp.where(row_head == h, kt, 0.0) for h in range(N_HEADS)], axis=1).astype(BF16)
            vexp_ref[j] = _head_expand(v_ref[0, pl.ds(start, tk), :], lane_head)
            return carry

        lax.fori_loop(0, n_kb, expand, 0)

    q = q_ref[0]
    row = lax.broadcasted_iota(jnp.int32, (tk, tk), 0)
    col = lax.broadcasted_iota(jnp.int32, (tk, tk), 1)
    tri = (row >= col).astype(BF16)

    n_blocks = (qi + 1) * per_q

    def key_block(j):
        return jnp.maximum(n_blocks - 1 - j, 0)

    def logits(j, slot):
        w_ref[slot] = _dot(q, kexp_ref[key_block(j)])

    def probs(slot, mask):
        neg_run = nr_ref[...]
        f_ref[slot] = jnp.exp2(neg_run)
        total = None
        for h in range(N_HEADS):
            w = w_ref[slot, :, h * tk:(h + 1) * tk]
            sp = jnp.maximum(jnp.log2(1.0 + jnp.exp2(jnp.minimum(w, SB_CLAMP))), w)
            if mask is not None:
                sp = jnp.where(mask, sp, 0.0)
            rev = _dot(sp.astype(BF16), tri)
            p = jnp.exp2(w - rev)
            if mask is not None:
                p = jnp.where(mask, p, 0.0)
            p_ref[slot, :, h * tk:(h + 1) * tk] = p.astype(BF16)
            total = rev[:, 0:1] if total is None else jnp.where(lane_head == h, rev[:, 0:1], total)
        nr_ref[...] = neg_run - total

    def accumulate(j, slot):
        acc_ref[...] += _dot(p_ref[slot], vexp_ref[key_block(j)]) * f_ref[slot]

    acc_ref[...] = jnp.zeros_like(acc_ref)
    nr_ref[...] = jnp.zeros_like(nr_ref)
    t_loc = lax.broadcasted_iota(jnp.int32, (tq, tk), 0)
    s_loc = lax.broadcasted_iota(jnp.int32, (tq, tk), 1)
    logits(0, 0)
    for j in range(per_q):
        logits(j + 1, (j + 1) % 2)
        probs(j % 2, ((per_q - 1 - j) * tk + s_loc) < t_loc)
        if j > 0:
            accumulate(j - 1, (j - 1) % 2)

    def pair(it, carry):
        j = per_q + 2 * it
        logits(j + 1, 1)
        probs(0, None)
        accumulate(j - 1, 1)
        logits(j + 2, 0)
        probs(1, None)
        accumulate(j, 0)
        return carry

    lax.fori_loop(0, qi * (per_q // 2), pair, 0)
    accumulate(n_blocks - 1, 1)
    o_ref[0] = acc_ref[...].astype(BF16)


def _stickbreak(u3):
    b, s, _ = u3.shape
    tq, tk = SB_QUERY_TILE, SB_KEY_TILE
    return pl.pallas_call(
        _stickbreak_body,
        grid=(b, s // tq),
        in_specs=[pl.BlockSpec((1, tq, D_BRANCH), lambda bi, i: (bi, i, COL_SB_Q)),
                  pl.BlockSpec((1, s, D_BRANCH), lambda bi, i: (bi, 0, COL_SB_K)),
                  pl.BlockSpec((1, s, D_BRANCH), lambda bi, i: (bi, 0, COL_SB_V))],
        out_specs=pl.BlockSpec((1, tq, D_BRANCH), lambda bi, i: (bi, i, 0)),
        out_shape=jax.ShapeDtypeStruct((b, s, D_BRANCH), BF16),
        scratch_shapes=[pltpu.VMEM((s // tk, D_BRANCH, N_HEADS * tk), BF16),
                        pltpu.VMEM((s // tk, N_HEADS * tk, D_BRANCH), BF16),
                        pltpu.VMEM((2, tq, N_HEADS * tk), F32),
                        pltpu.VMEM((2, tq, N_HEADS * tk), BF16),
                        pltpu.VMEM((2, tq, D_BRANCH), F32),
                        pltpu.VMEM((tq, D_BRANCH), F32),
                        pltpu.VMEM((tq, D_BRANCH), F32)],
        compiler_params=_params("arbitrary", "arbitrary"),
        name="stickbreak",
    )(u3, u3, u3)


def _rope_body(pos_ref, f_ref, cos_ref, sin_ref):
    ang = pos_ref[0].astype(F32) * f_ref[...]
    cos_ref[0] = jnp.cos(ang)
    sin_ref[0] = jnp.sin(ang)


def _rope_tables(positions):
    b, s = positions.shape
    half = HEAD_DIM // 2
    inv_freq = ROPE_THETA ** (-jnp.arange(half, dtype=F32) / half)
    freq = jnp.tile(inv_freq, 128 // half)[None, :]
    ts = SEQ_TILE
    out = jax.ShapeDtypeStruct((b, s, 128), F32)
    return pl.pallas_call(
        _rope_body,
        grid=(b, s // ts),
        in_specs=[pl.BlockSpec((1, ts, 1), lambda bi, i: (bi, i, 0)), _resident((1, 128))],
        out_specs=[pl.BlockSpec((1, ts, 128), lambda bi, i: (bi, i, 0))] * 2,
        out_shape=[out, out],
        compiler_params=_params("parallel", "parallel"),
        name="rope_tables",
    )(positions[:, :, None], freq)


def _retention_body(q_ref, k_ref, v_ref, g_ref, cos_ref, sin_ref, dec_ref, xi_ref, zeta_ref,
                    gam_ref, o_ref, state_ref):
    @pl.when(pl.program_id(1) == 0)
    def _():
        state_ref[...] = jnp.zeros_like(state_ref)

    lane = lax.broadcasted_iota(jnp.int32, (1, D_BRANCH), 1)
    lane_head = lax.shift_right_logical(lane, 6)
    first_half = (lane & (HEAD_DIM - 1)) < HEAD_DIM // 2
    cos = jnp.concatenate([cos_ref[0], cos_ref[0]], axis=1)
    sin = jnp.concatenate([sin_ref[0], sin_ref[0]], axis=1)

    def rope(x):
        partner = jnp.where(first_half, -pltpu.roll(x, D_BRANCH - HEAD_DIM // 2, 1),
                            pltpu.roll(x, HEAD_DIM // 2, 1))
        return x * cos + partner * sin

    q = rope(q_ref[0].astype(F32))
    k = rope(k_ref[0].astype(F32)) * HEAD_DIM ** -0.5
    v = v_ref[0]
    kexp = _head_expand(k.astype(BF16), lane_head)
    vexp = _head_expand(v, lane_head)
    scores = _dot_nt(q.astype(BF16), kexp) * dec_ref[...]
    inner = _dot(scores.astype(BF16), vexp)
    state = state_ref[...]
    cross = _dot((q * xi_ref[...]).astype(BF16), state.astype(BF16))
    update = _dot_tn((k * zeta_ref[...]).astype(BF16), v)
    r = lax.broadcasted_iota(jnp.int32, (D_BRANCH, D_BRANCH), 0)
    c = lax.broadcasted_iota(jnp.int32, (D_BRANCH, D_BRANCH), 1)
    same_head = lax.shift_right_logical(r, 6) == lax.shift_right_logical(c, 6)
    state_ref[...] = state * gam_ref[...] + jnp.where(same_head, update, 0.0)

    o = inner + cross
    gm = _group_mean_matrix()
    oc = o - _split_dot(o, gm)
    var = _split_dot(oc * oc, gm)
    o_ref[0] = (_silu(g_ref[0].astype(F32)) * (oc * lax.rsqrt(var + EPS))).astype(BF16)


def _retention(u3, cos_t, sin_t):
    b, s, _ = u3.shape
    tc = SEQ_TILE
    gammas = 1.0 - jnp.exp2(-5.0 - jnp.arange(N_HEADS, dtype=F32))
    log_g = jnp.log(gammas)
    log_g_lane = jnp.repeat(log_g, HEAD_DIM)[None, :]
    idx = jnp.arange(tc, dtype=F32)
    rel = idx[:, None] - idx[None, :]
    decay = jnp.where(rel >= 0, jnp.exp(jnp.maximum(rel, 0.0) * log_g[:, None, None]), 0.0)
    decay = jnp.transpose(decay, (1, 0, 2)).reshape(tc, N_HEADS * tc)
    xi = jnp.exp((idx + 1.0)[:, None] * log_g_lane)
    zeta = jnp.exp((tc - 1.0 - idx)[:, None] * log_g_lane)
    gam = jnp.broadcast_to(jnp.exp(tc * log_g_lane).T, (D_BRANCH, D_BRANCH))

    def col(cb):
        return pl.BlockSpec((1, tc, D_BRANCH), lambda bi, i: (bi, i, cb))

    tab = pl.BlockSpec((1, tc, 128), lambda bi, i: (bi, i, 0))
    return pl.pallas_call(
        _retention_body,
        grid=(b, s // tc),
        in_specs=[col(COL_RET_Q), col(COL_RET_K), col(COL_RET_V), col(COL_RET_G), tab, tab,
                  _resident((tc, N_HEADS * tc)), _resident((tc, D_BRANCH)),
                  _resident((tc, D_BRANCH)), _resident((D_BRANCH, D_BRANCH))],
        out_specs=pl.BlockSpec((1, tc, D_BRANCH), lambda bi, i: (bi, i, 0)),
        out_shape=jax.ShapeDtypeStruct((b, s, D_BRANCH), BF16),
        scratch_shapes=[pltpu.VMEM((D_BRANCH, D_BRANCH), F32)],
        compiler_params=_params("parallel", "arbitrary"),
        name="retention",
    )(u3, u3, u3, u3, cos_t, sin_t, decay, xi, zeta, gam)


def _merge_body(x_ref, ycp_ref, ysb_ref, yret_ref, g_ref, wg_ref, wb_ref, wo_ref, o_ref):
    x = x_ref[...]
    h = _rms_rows(x, g_ref[...]).astype(BF16)
    ycp = ycp_ref[...]
    branches = (ycp[:, :D_BRANCH], ycp[:, D_BRANCH:], ysb_ref[...], yret_ref[...])
    parts = []
    for n in range(D_MODEL // D_BRANCH):
        cols = slice(n * D_BRANCH, (n + 1) * D_BRANCH)
        m = None
        for i, y in enumerate(branches):
            term = _sigmoid(_dot(h, wg_ref[i, :, cols])) * _dot(y, wb_ref[i, :, cols])
            m = term if m is None else m + term
        parts.append(m.astype(BF16))
    o_ref[...] = x + _dot(jnp.concatenate(parts, axis=1), wo_ref[...])


def _merge(x2, ycp, ysb, yret, g, wg, wb, wo):
    t = x2.shape[0]
    tm = TOKEN_TILE

    def rows(width):
        return pl.BlockSpec((tm, width), lambda i: (i, 0))

    return pl.pallas_call(
        _merge_body,
        grid=(t // tm,),
        in_specs=[rows(D_MODEL), rows(2 * D_BRANCH), rows(D_BRANCH), rows(D_BRANCH),
                  _resident((1, D_MODEL)), _resident((N_HEADS, D_MODEL, D_MODEL)),
                  _resident((N_HEADS, D_BRANCH, D_MODEL)), _resident((D_MODEL, D_MODEL))],
        out_specs=rows(D_MODEL),
        out_shape=jax.ShapeDtypeStruct((t, D_MODEL), F32),
        compiler_params=_params("parallel"),
        name="merge",
    )(x2, ycp, ysb, yret, g, wg, wb, wo)


def _head_rms(x, gain_lanes, gm):
    ms = _split_dot(x * x, gm)
    return x * lax.rsqrt(ms + EPS) * gain_lanes


def _mem_kv_body(m_ref, g_ref, w_ref, gk_ref, k_ref, v_ref):
    hm = _rms_rows(m_ref[0], g_ref[...]).astype(BF16)
    kv = _dot(hm, w_ref[...])
    k_ref[0] = _head_rms(kv[:, :D_BRANCH], gk_ref[...], _group_mean_matrix()).astype(BF16)
    v_ref[0] = kv[:, D_BRANCH:].astype(BF16)


def _mem_kv(mem, g, wkv, gk_lanes):
    b, m, _ = mem.shape
    out = jax.ShapeDtypeStruct((b, m, D_BRANCH), BF16)
    blk = pl.BlockSpec((1, m, D_BRANCH), lambda bi: (bi, 0, 0))
    return pl.pallas_call(
        _mem_kv_body,
        grid=(b,),
        in_specs=[pl.BlockSpec((1, m, D_MODEL), lambda bi: (bi, 0, 0)), _resident((1, D_MODEL)),
                  _resident((D_MODEL, 2 * D_BRANCH)), _resident((1, D_BRANCH))],
        out_specs=[blk, blk],
        out_shape=[out, out],
        compiler_params=_params("parallel"),
        name="mem_kv",
    )(mem, g, wkv, gk_lanes)


def _xattn_body(x_ref, k_ref, v_ref, g_ref, wq_ref, gq_ref, wo_ref, o_ref):
    x = x_ref[0]
    m = k_ref.shape[1]
    h = _rms_rows(x, g_ref[...]).astype(BF16)
    q = _head_rms(_dot(h, wq_ref[...]), gq_ref[...], _group_mean_matrix()).astype(BF16)
    lane_head = _lane_head()
    s_all = _dot_nt(q, _head_expand(k_ref[0], lane_head)) * HEAD_DIM ** -0.5
    probs = []
    for hd in range(N_HEADS):
        s = s_all[:, hd * m:(hd + 1) * m]
        e = jnp.exp(s - jnp.max(s, axis=-1, keepdims=True))
        probs.append((e / jnp.sum(e, axis=-1, keepdims=True)).astype(BF16))
    o = _dot(jnp.concatenate(probs, axis=1), _head_expand(v_ref[0], lane_head))
    o_ref[0] = x + _dot(o.astype(BF16), wo_ref[...])


def _xattn(x3, k, v, g, wq, gq_lanes, wo):
    b, s, _ = x3.shape
    m = k.shape[1]
    ts = TOKEN_TILE
    xblk = pl.BlockSpec((1, ts, D_MODEL), lambda bi, i: (bi, i, 0))
    kvblk = pl.BlockSpec((1, m, D_BRANCH), lambda bi, i: (bi, 0, 0))
    return pl.pallas_call(
        _xattn_body,
        grid=(b, s // ts),
        in_specs=[xblk, kvblk, kvblk, _resident((1, D_MODEL)), _resident((D_MODEL, D_BRANCH)),
                  _resident((1, D_BRANCH)), _resident((D_BRANCH, D_MODEL))],
        out_specs=xblk,
        out_shape=jax.ShapeDtypeStruct((b, s, D_MODEL), F32),
        compiler_params=_params("parallel", "parallel"),
        name="xattn",
    )(x3, k, v, g, wq, gq_lanes, wo)


def _swiglu_rows(h, wgu_ref, wd_ref, acc_ref, between=None):
    for j in range(D_FF // FF_CHUNK):
        cols = slice(j * FF_CHUNK, (j + 1) * FF_CHUNK)
        up_cols = slice(D_FF + j * FF_CHUNK, D_FF + (j + 1) * FF_CHUNK)
        a = (_silu(_dot(h, wgu_ref[:, cols])) * _dot(h, wgu_ref[:, up_cols])).astype(BF16)
        part = _dot(a, wd_ref[cols, :])
        if j == 0:
            acc_ref[...] = part
        else:
            acc_ref[...] += part
        if between is not None:
            between(j)


def _ffn_body(x_ref, g_ref, wgu_ref, wd_ref, o_ref, acc_ref):
    x = x_ref[...]
    h = _rms_rows(x, g_ref[...]).astype(BF16)
    _swiglu_rows(h, wgu_ref, wd_ref, acc_ref)
    o_ref[...] = x + acc_ref[...]


def _ffn(x2, g, wgu, wd):
    t = x2.shape[0]
    tm = TOKEN_TILE
    rows = pl.BlockSpec((tm, D_MODEL), lambda i: (i, 0))
    return pl.pallas_call(
        _ffn_body,
        grid=(t // tm,),
        in_specs=[rows, _resident((1, D_MODEL)), _resident((D_MODEL, 2 * D_FF)),
                  _resident((D_FF, D_MODEL))],
        out_specs=rows,
        out_shape=jax.ShapeDtypeStruct((t, D_MODEL), F32),
        scratch_shapes=[pltpu.VMEM((tm, D_MODEL), F32)],
        compiler_params=_params("parallel"),
        name="ffn",
    )(x2, g, wgu, wd)


LANE_TILES = D_MODEL // 128


def _to_token_tiles(x, o_ref):
    n = x.shape[0]
    for c in range(LANE_TILES):
        o_ref[pl.ds(c, n, stride=LANE_TILES), :] = x[:, c * 128:(c + 1) * 128]


def _from_token_tiles(ref, n):
    return jnp.concatenate([ref[pl.ds(c, n, stride=LANE_TILES), :] for c in range(LANE_TILES)], axis=1)


def _router_body(x_ref, g_ref, rhi_ref, rlo_ref, o_ref, xt_ref):
    x = x_ref[...]
    _to_token_tiles(x, xt_ref)
    h = _rms_rows(x, g_ref[...])
    hi = h.astype(BF16)
    lo = (h - hi.astype(F32)).astype(BF16)
    logits = _dot(hi, rhi_ref[...]) + (_dot(hi, rlo_ref[...]) + _dot(lo, rhi_ref[...]))
    lane = lax.broadcasted_iota(jnp.int32, logits.shape, 1)
    neg = jnp.float32(-jnp.inf)
    logits = jnp.where(lane < N_EXPERTS, logits, neg)
    m1 = jnp.max(logits, axis=-1, keepdims=True)
    i1 = jnp.min(jnp.where(logits == m1, lane, ROUTER_LANES), axis=-1, keepdims=True)
    rest = jnp.where(lane == i1, neg, logits)
    m2 = jnp.max(rest, axis=-1, keepdims=True)
    i2 = jnp.min(jnp.where(rest == m2, lane, ROUTER_LANES), axis=-1, keepdims=True)
    e2 = jnp.exp(m2 - m1)
    w1 = 1.0 / (1.0 + e2)
    w2 = e2 / (1.0 + e2)
    out = jnp.where(lane == 0, i1.astype(F32), 0.0)
    out = jnp.where(lane == 1, i2.astype(F32), out)
    out = jnp.where(lane == 2, w1, out)
    o_ref[...] = jnp.where(lane == 3, w2, out)


def _router(x2, g, r_hi, r_lo):
    t = x2.shape[0]
    tm = TOKEN_TILE
    return pl.pallas_call(
        _router_body,
        grid=(t // tm,),
        in_specs=[pl.BlockSpec((tm, D_MODEL), lambda i: (i, 0)), _resident((1, D_MODEL)),
                  _resident((D_MODEL, ROUTER_LANES)), _resident((D_MODEL, ROUTER_LANES))],
        out_specs=[pl.BlockSpec((tm, ROUTER_LANES), lambda i: (i, 0)),
                   pl.BlockSpec((tm * LANE_TILES, 128), lambda i: (i, 0))],
        out_shape=[jax.ShapeDtypeStruct((t, ROUTER_LANES), F32),
                   jax.ShapeDtypeStruct((t * LANE_TILES, 128), F32)],
        compiler_params=_params("parallel"),
        name="router",
    )(x2, g, r_hi, r_lo)


def _experts_body(tile_expert_ref, n_tiles_ref, src_ref, src_next_ref, dst_prev_ref, dst_ref,
                  x_hbm, g_ref, wgu_ref, wd_ref, y_hbm, xbuf_ref, ybuf_ref, acc_ref, gsem, ssem):
    i = pl.program_id(0)
    last = pl.num_programs(0) - 1
    slot = lax.rem(i, 2)
    other = 1 - slot
    tm = EXPERT_TILE
    rows = LANE_TILES

    def gather(idx_ref, r, s):
        src = pl.multiple_of(idx_ref[0, 0, r] * rows, rows)
        return pltpu.make_async_copy(x_hbm.at[pl.ds(src, rows)],
                                     xbuf_ref.at[s, pl.ds(r * rows, rows)], gsem)

    def scatter(idx_ref, r, s):
        dst = pl.multiple_of(idx_ref[0, 0, r] * rows, rows)
        return pltpu.make_async_copy(ybuf_ref.at[s, pl.ds(r * rows, rows)],
                                     y_hbm.at[pl.ds(dst, rows)], ssem)

    def wait_gather():
        pltpu.make_async_copy(x_hbm.at[pl.ds(0, tm * rows)], xbuf_ref.at[0], gsem).wait()

    def wait_scatter():
        pltpu.make_async_copy(ybuf_ref.at[0], y_hbm.at[pl.ds(0, tm * rows)], ssem).wait()

    def start_next(r, lane=0):
        gather(src_next_ref, r, other).start(priority=lane)
        scatter(dst_prev_ref, r, other).start(priority=1 - lane)

    @pl.when(i == 0)
    def _():
        ybuf_ref[...] = jnp.zeros_like(ybuf_ref)
        lax.fori_loop(0, tm, lambda r, c: (gather(src_ref, r, 0).start(), c)[1], 0)

    wait_gather()

    @pl.when(i > 0)
    def _():
        wait_scatter()

    @pl.when(i < n_tiles_ref[0])
    def _():
        h = _rms_rows(_from_token_tiles(xbuf_ref.at[slot], tm), g_ref[...]).astype(BF16)
        n_chunks = D_FF // FF_CHUNK
        per = -(-tm // n_chunks)

        def between(j):
            for r in range(j * per, min((j + 1) * per, tm)):
                start_next(r, r % 2)

        _swiglu_rows(h, wgu_ref.at[0], wd_ref.at[0], acc_ref, between)
        _to_token_tiles(acc_ref[...], ybuf_ref.at[slot])

    @pl.when(i >= n_tiles_ref[0])
    def _():
        ybuf_ref[slot] = jnp.zeros((tm * rows, 128), F32)
        lax.fori_loop(0, tm, lambda r, c: (start_next(r), c)[1], 0)

    @pl.when(i == last)
    def _():
        wait_gather()
        wait_scatter()
        lax.fori_loop(0, tm, lambda r, c: (scatter(dst_ref, r, slot).start(), c)[1], 0)
        wait_scatter()


def _experts(x_tiles, src_tiles, dst_tiles, tile_expert, n_tiles, g, wgu, wd):
    n_steps = src_tiles.shape[0] - 1
    tm = EXPERT_TILE

    def idx(off):
        return pl.BlockSpec((1, 1, tm), lambda i, te, nt: (i + off, 0, 0), memory_space=pltpu.SMEM)

    def expert(shape):
        return pl.BlockSpec((1,) + shape, lambda i, te, nt: (te[i], 0, 0),
                            pipeline_mode=pl.Buffered(1))

    grid_spec = pltpu.PrefetchScalarGridSpec(
        num_scalar_prefetch=2,
        grid=(n_steps,),
        in_specs=[idx(0), idx(1), idx(0), idx(1),
                  pl.BlockSpec(memory_space=pl.ANY),
                  pl.BlockSpec((1, D_MODEL), lambda i, te, nt: (0, 0)),
                  expert((D_MODEL, 2 * D_FF)), expert((D_FF, D_MODEL))],
        out_specs=pl.BlockSpec(memory_space=pl.ANY),
        scratch_shapes=[pltpu.VMEM((2, tm * LANE_TILES, 128), F32),
                        pltpu.VMEM((2, tm * LANE_TILES, 128), F32),
                        pltpu.VMEM((tm, D_MODEL), F32),
                        pltpu.SemaphoreType.DMA(()), pltpu.SemaphoreType.DMA(())],
    )
    return pl.pallas_call(
        _experts_body,
        grid_spec=grid_spec,
        out_shape=jax.ShapeDtypeStruct(((n_steps + 1) * tm * LANE_TILES, 128), F32),
        compiler_params=_params("arbitrary"),
        name="experts",
    )(tile_expert, n_tiles, src_tiles, src_tiles, dst_tiles, dst_tiles, x_tiles, g, wgu, wd)


def _combine_body(x_ref, y0_ref, y1_ref, r_ref, o_ref):
    routed = r_ref[...]
    tm = x_ref.shape[0]
    y0 = _from_token_tiles(y0_ref, tm)
    y1 = _from_token_tiles(y1_ref, tm)
    o_ref[...] = x_ref[...] + (routed[:, 2:3] * y0 + routed[:, 3:4] * y1)


def _combine(x2, y_tiles, routed):
    t = x2.shape[0]
    tm = TOKEN_TILE
    rows = pl.BlockSpec((tm, D_MODEL), lambda i: (i, 0))
    return pl.pallas_call(
        _combine_body,
        grid=(t // tm,),
        in_specs=[rows, pl.BlockSpec((tm * LANE_TILES, 128), lambda i: (i, 0)),
                  pl.BlockSpec((tm * LANE_TILES, 128), lambda i: (i + t // tm, 0)),
                  pl.BlockSpec((tm, ROUTER_LANES), lambda i: (i, 0))],
        out_specs=rows,
        out_shape=jax.ShapeDtypeStruct((t, D_MODEL), F32),
        compiler_params=_params("parallel"),
        name="combine",
    )(x2, y_tiles, y_tiles, routed)


def _moe(x2, g, router, wgu, wd):
    t = x2.shape[0]
    tm = EXPERT_TILE
    r_pad = jnp.pad(router, ((0, 0), (0, ROUTER_LANES - N_EXPERTS)))
    r_hi = r_pad.astype(BF16)
    r_lo = (r_pad - r_hi.astype(F32)).astype(BF16)
    routed, x_tiles = _router(x2, g, r_hi, r_lo)
    expert_of = routed[:, 0:2].astype(jnp.int32).T.reshape(-1)

    n_steps = (2 * t) // tm + N_EXPERTS
    n_pad = n_steps * tm - 2 * t
    counts = jnp.sum((expert_of[:, None] == jnp.arange(N_EXPERTS, dtype=jnp.int32)[None, :])
                     .astype(jnp.int32), axis=0)
    padded = ((counts + tm - 1) // tm) * tm
    pad_ends = jnp.cumsum(padded - counts)
    filler = jnp.arange(n_pad, dtype=jnp.int32)
    filler_expert = jnp.sum((filler[:, None] >= pad_ends[None, :]).astype(jnp.int32), axis=1)
    shift = 1 + (2 * t - 1).bit_length()
    keys = jnp.concatenate([expert_of * (1 << shift) + jnp.arange(2 * t, dtype=jnp.int32),
                            filler_expert * (1 << shift) + (1 << (shift - 1)) + filler])
    vals = jnp.concatenate([jnp.arange(2 * t, dtype=jnp.int32), 2 * t + tm + filler])
    _, dst = lax.sort((keys, vals), num_keys=1)
    src = jnp.where(dst < 2 * t, dst % t, 0)
    spare_tile = 2 * t + jnp.arange(tm, dtype=jnp.int32)
    dst_tiles = jnp.concatenate([spare_tile, dst]).reshape(n_steps + 1, 1, tm)
    src_tiles = jnp.concatenate([src, jnp.zeros((tm,), jnp.int32)]).reshape(n_steps + 1, 1, tm)
    ends = jnp.cumsum(padded)
    n_tiles = (ends[-1] // tm).astype(jnp.int32).reshape(1)
    tile_start = jnp.minimum(jnp.arange(n_steps, dtype=jnp.int32), n_tiles[0] - 1) * tm
    tile_expert = jnp.sum((tile_start[:, None] >= ends[None, :]).astype(jnp.int32), axis=1)

    y_tiles = _experts(x_tiles, src_tiles, dst_tiles, tile_expert, n_tiles, g, wgu, wd)
    return _combine(x2, y_tiles, routed)


def _block_diag(w):
    g, n, _ = w.shape
    eye = jnp.eye(g, dtype=w.dtype)
    return (eye[:, None, :, None] * w[:, :, None, :]).reshape(g * n, g * n)


def kernel(x, mem, positions, norm_mix_g, w_in, conv_w, conv_b, conv_ln_g, conv_ln_b, pool_w, pool_scale, w_gate, w_branch, w_out, norm_xa_g, norm_mem_g, xa_wq, xa_wkv, xa_q_norm_g, xa_k_norm_g, xa_wo, norm_ffn_g, ffn_w_gu, ffn_w_down, moe_router, moe_w_gu, moe_w_down):
    b, s, d = x.shape
    depth = w_in.shape[0]
    t = b * s
    assert d == D_MODEL and s % TOKEN_TILE == 0 and s % SEQ_TILE == 0

    def row(v):
        return v.astype(F32)[None, :]

    cos_t, sin_t = _rope_tables(positions)
    x2 = x.astype(F32).reshape(t, d)
    for l in range(depth):
        cw = jnp.pad(conv_w[l].astype(F32), ((0, SEQ_HALO - CONV_WIDTH), (0, 0)))
        ycp, u_att = _mix_front(x2.reshape(b, s, d), row(norm_mix_g[l]), w_in[l].astype(BF16), cw,
                                row(conv_b[l]), row(conv_ln_g[l]), row(conv_ln_b[l]),
                                _block_diag(pool_w[l]).astype(BF16), row(pool_scale[l]))
        ysb = _stickbreak(u_att)
        yret = _retention(u_att, cos_t, sin_t)
        x2 = _merge(x2, ycp.reshape(t, 2 * D_BRANCH), ysb.reshape(t, D_BRANCH),
                    yret.reshape(t, D_BRANCH), row(norm_mix_g[l]), w_gate[l].astype(BF16),
                    w_branch[l].astype(BF16), w_out[l].astype(BF16))
        k_mem, v_mem = _mem_kv(mem.astype(F32), row(norm_mem_g[l]), xa_wkv[l].astype(BF16),
                               row(jnp.tile(xa_k_norm_g[l], N_HEADS)))
        x2 = _xattn(x2.reshape(b, s, d), k_mem, v_mem, row(norm_xa_g[l]), xa_wq[l].astype(BF16),
                    row(jnp.tile(xa_q_norm_g[l], N_HEADS)), xa_wo[l].astype(BF16)).reshape(t, d)
        g_ffn = row(norm_ffn_g[l])
        if l % 2 == 0:
            x2 = _ffn(x2, g_ffn, ffn_w_gu[l // 2].astype(BF16), ffn_w_down[l // 2].astype(BF16))
        else:
            x2 = _moe(x2, g_ffn, moe_router[l // 2].astype(F32), moe_w_gu[l // 2].astype(BF16),
                      moe_w_down[l // 2].astype(BF16))
    return x2.reshape(b, s, d).astype(x.dtype)
```

```python
import math

import jax
import jax.numpy as jnp
from jax import lax
from jax.experimental import pallas as pl
from jax.experimental.pallas import tpu as pltpu

F32 = jnp.float32
BF16 = jnp.bfloat16

D_MODEL = 1024
HEAD_DIM = 64
N_HEADS = 4
D_BRANCH = N_HEADS * HEAD_DIM
CONV_WIDTH = 31
POOL_WINDOWS = (2, 4, 8, 16)
D_FF = 2816
N_EXPERTS = 8
ROPE_THETA = 10000.0
EPS = 1e-6

COL_CONV = 0
COL_POOL = 2
D_IN = 10 * D_BRANCH
COL_SB_Q, COL_SB_K, COL_SB_V = 0, 1, 2
COL_RET_Q, COL_RET_K, COL_RET_V, COL_RET_G = 3, 4, 5, 6
D_ATT = 7 * D_BRANCH

V7X_VMEM_BYTES = 64 * 1024 * 1024
VMEM_LIMIT_BYTES = V7X_VMEM_BYTES - 8 * 1024 * 1024

TOKEN_TILE = 512
SEQ_TILE = 256
SB_QUERY_TILE = 512
SB_KEY_TILE = 256
SB_CLAMP = 30.0
LOG2E = 1.4426950408889634
RET_BATCH = 4
SEQ_HALO = 32
CONV_ROWS = 64
FF_CHUNK = 256
EXPERT_TILE = 512
EXPERT_ISSUE_CHUNKS = 6
ROUTER_LANES = 128


def _resident(shape):
    return pl.BlockSpec(shape, lambda *_: (0,) * len(shape), pipeline_mode=pl.Buffered(1))


def _params(*semantics):
    return pltpu.CompilerParams(dimension_semantics=semantics, vmem_limit_bytes=VMEM_LIMIT_BYTES)


def _sigmoid(x):
    return 0.5 * jnp.tanh(0.5 * x) + 0.5


def _silu(x):
    return x * _sigmoid(x)


def _rms_rows(x, g):
    ms = jnp.mean(x * x, axis=-1, keepdims=True)
    return x * lax.rsqrt(ms + EPS) * g


def _dot(a, b):
    return jnp.dot(a, b, preferred_element_type=F32)


def _dot_nt(a, b):
    return lax.dot_general(a, b, (((1,), (1,)), ((), ())), preferred_element_type=F32)


def _dot_tn(a, b):
    return lax.dot_general(a, b, (((0,), (0,)), ((), ())), preferred_element_type=F32)


def _split_dot(x, w):
    hi = x.astype(BF16)
    lo = (x - hi.astype(F32)).astype(BF16)
    return _dot(hi, w) + _dot(lo, w)


def _head_expand(x, lane_head):
    zero = jnp.zeros_like(x)
    return jnp.concatenate([jnp.where(lane_head == h, x, zero) for h in range(N_HEADS)], axis=0)


def _lane_head(width=D_BRANCH):
    return lax.shift_right_logical(lax.broadcasted_iota(jnp.int32, (1, width), 1), 6)


def _group_mean_matrix():
    r = lax.broadcasted_iota(jnp.int32, (D_BRANCH, D_BRANCH), 0)
    c = lax.broadcasted_iota(jnp.int32, (D_BRANCH, D_BRANCH), 1)
    same = lax.shift_right_logical(r, 6) == lax.shift_right_logical(c, 6)
    return jnp.where(same, 1.0 / HEAD_DIM, 0.0).astype(BF16)


def _mix_front_body(x_ref, g_ref, w_ref, cw_ref, cb_ref, lg_ref, lb_ref, pw_ref, ps_ref,
                    y_ref, ua_ref, vs_ref, sh_ref, co_ref, pp_ref, q_ref):
    ts = x_ref.shape[1]
    i = pl.program_id(1)
    h = _rms_rows(x_ref[0], g_ref[...]).astype(BF16)
    att = (COL_POOL + 1) * D_BRANCH
    u = _dot(h, w_ref[:, 0:att])
    ua = _dot(h, w_ref[:, att:])
    ua_ref[0, :, 0:D_BRANCH] = (ua[:, 0:D_BRANCH] * (LOG2E * HEAD_DIM ** -0.5)).astype(BF16)
    ua_ref[0, :, D_BRANCH:] = ua[:, D_BRANCH:].astype(BF16)

    @pl.when(i == 0)
    def _():
        vs_ref[0:SEQ_HALO, :] = jnp.zeros((SEQ_HALO, D_BRANCH), F32)
        pp_ref[0:SEQ_HALO, :] = jnp.zeros((SEQ_HALO, D_BRANCH), F32)

    vs_ref[SEQ_HALO:SEQ_HALO + ts, :] = u[:, 0:D_BRANCH] * _sigmoid(u[:, D_BRANCH:2 * D_BRANCH])
    span = ts + SEQ_HALO - 8
    for r in range(1, 8):
        sh_ref[r, 0:span, :] = vs_ref[r:r + span, :]
    lead = SEQ_HALO - (CONV_WIDTH - 1)
    for r0 in range(0, ts, CONV_ROWS):
        acc = jnp.zeros((CONV_ROWS, D_BRANCH), F32)
        for k in range(CONV_WIDTH):
            a, r = divmod(lead + k, 8)
            rows = slice(r0 + 8 * a, r0 + 8 * a + CONV_ROWS)
            tap = vs_ref[rows, :] if r == 0 else sh_ref[r, rows, :]
            acc = acc + cw_ref[k:k + 1, :] * tap
        co_ref[r0:r0 + CONV_ROWS, :] = acc
    c = co_ref[...] + cb_ref[...]
    mu = jnp.mean(c, axis=-1, keepdims=True)
    cc = c - mu
    var = jnp.mean(cc * cc, axis=-1, keepdims=True)
    y_ref[0, :, 0:D_BRANCH] = _silu(cc * lax.rsqrt(var + EPS) * lg_ref[...] + lb_ref[...]).astype(BF16)
    vs_ref[0:SEQ_HALO, :] = vs_ref[ts:ts + SEQ_HALO, :]

    up = u[:, COL_POOL * D_BRANCH:att]
    pp_ref[SEQ_HALO:SEQ_HALO + ts, :] = up
    end = SEQ_HALO + ts
    t_pos = (i * ts + lax.broadcasted_iota(jnp.int32, (ts, 1), 0)).astype(F32) + 1.0
    lane = lax.broadcasted_iota(jnp.int32, (1, 128), 1)
    means = []
    for half in range(2):
        cols = slice(half * 128, (half + 1) * 128)
        q_ref[0, 8:end, cols] = pp_ref[8:end, cols] + pp_ref[7:end - 1, cols]
        q_ref[1, 16:end, cols] = q_ref[0, 16:end, cols] + q_ref[0, 14:end - 2, cols]
        if half == 0:
            small, big = q_ref[0, SEQ_HALO:end, cols], q_ref[1, SEQ_HALO:end, cols]
        else:
            q_ref[2, 24:end, cols] = q_ref[1, 24:end, cols] + q_ref[1, 20:end - 4, cols]
            small = q_ref[2, SEQ_HALO:end, cols]
            big = small + q_ref[2, SEQ_HALO - 8:end - 8, cols]
        w_small, w_big = POOL_WINDOWS[2 * half], POOL_WINDOWS[2 * half + 1]
        m_small = small / jnp.minimum(t_pos, float(w_small))
        m_big = big / jnp.minimum(t_pos, float(w_big))
        means.append(jnp.where(lane < HEAD_DIM, m_small, m_big))
    d = (jnp.concatenate(means, axis=1) - up).astype(BF16)
    y_ref[0, :, D_BRANCH:2 * D_BRANCH] = (_dot(d, pw_ref[...]) * ps_ref[...]).astype(BF16)
    pp_ref[0:SEQ_HALO, :] = pp_ref[ts:ts + SEQ_HALO, :]


def _mix_front(x3, g, w, conv_w, conv_b, ln_g, ln_b, pool_w_bd, pool_scale):
    b, s, _ = x3.shape
    ts = TOKEN_TILE
    return pl.pallas_call(
        _mix_front_body,
        grid=(b, s // ts),
        in_specs=[pl.BlockSpec((1, ts, D_MODEL), lambda bi, i: (bi, i, 0)),
                  _resident((1, D_MODEL)), _resident((D_MODEL, D_IN)),
                  _resident((SEQ_HALO, D_BRANCH)), _resident((1, D_BRANCH)),
                  _resident((1, D_BRANCH)), _resident((1, D_BRANCH)),
                  _resident((D_BRANCH, D_BRANCH)), _resident((1, D_BRANCH))],
        out_specs=[pl.BlockSpec((1, ts, 2 * D_BRANCH), lambda bi, i: (bi, i, 0)),
                   pl.BlockSpec((1, ts, D_ATT), lambda bi, i: (bi, i, 0))],
        out_shape=[jax.ShapeDtypeStruct((b, s, 2 * D_BRANCH), BF16),
                   jax.ShapeDtypeStruct((b, s, D_ATT), BF16)],
        scratch_shapes=[pltpu.VMEM((SEQ_HALO + ts, D_BRANCH), F32),
                        pltpu.VMEM((8, SEQ_HALO + ts, D_BRANCH), F32),
                        pltpu.VMEM((ts, D_BRANCH), F32),
                        pltpu.VMEM((SEQ_HALO + ts, D_BRANCH), F32),
                        pltpu.VMEM((3, SEQ_HALO + ts, D_BRANCH), F32)],
        compiler_params=_params("arbitrary", "arbitrary"),
        name="mix_front",
    )(x3, g, w, conv_w, conv_b, ln_g, ln_b, pool_w_bd, pool_scale)


def _stickbreak_body(q_ref, k_ref, v_ref, o_ref, kexp_ref, vexp_ref, w_ref, p_ref, f_ref, nr_ref,
                     acc_ref):
    tq = q_ref.shape[1]
    tk = SB_KEY_TILE
    n_kb = k_ref.shape[1] // tk
    per_q = tq // tk
    assert per_q % 2 == 0
    qi = pl.program_id(1)
    lane_head = _lane_head()

    @pl.when(qi == 0)
    def _():
        row_head = lax.shift_right_logical(lax.broadcasted_iota(jnp.int32, (D_BRANCH, 1), 0), 6)

        def expand(j, carry):
            start = pl.multiple_of(j * tk, tk)
            kt = k_ref[0, pl.ds(start, tk), :].astype(F32).T
            kexp_ref[j] = jnp.concatenate(
                [jnp.where(row_head == h, kt, 0.0) for h in range(N_HEADS)], axis=1).astype(BF16)
            vexp_ref[j] = _head_expand(v_ref[0, pl.ds(start, tk), :], lane_head)
            return carry

        lax.fori_loop(0, n_kb, expand, 0)

    q = q_ref[0]
    row = lax.broadcasted_iota(jnp.int32, (tk, tk), 0)
    col = lax.broadcasted_iota(jnp.int32, (tk, tk), 1)
    tri = (row >= col).astype(BF16)

    n_blocks = (qi + 1) * per_q

    def key_block(j):
        return jnp.maximum(n_blocks - 1 - j, 0)

    def logits(j, slot):
        w_ref[slot] = _dot(q, kexp_ref[key_block(j)])

    def probs(slot, mask):
        neg_run = nr_ref[...]
        f_ref[slot] = jnp.exp2(neg_run)
        total = None
        for h in range(N_HEADS):
            w = w_ref[slot, :, h * tk:(h + 1) * tk]
            sp = jnp.maximum(jnp.log2(1.0 + jnp.exp2(jnp.minimum(w, SB_CLAMP))), w)
            if mask is not None:
                sp = jnp.where(mask, sp, 0.0)
            rev = _dot(sp.astype(BF16), tri)
            p = jnp.exp2(w - rev)
            if mask is not None:
                p = jnp.where(mask, p, 0.0)
            p_ref[slot, :, h * tk:(h + 1) * tk] = p.astype(BF16)
            total = rev[:, 0:1] if total is None else jnp.where(lane_head == h, rev[:, 0:1], total)
        nr_ref[...] = neg_run - total

    def accumulate(j, slot):
        acc_ref[...] += _dot(p_ref[slot], vexp_ref[key_block(j)]) * f_ref[slot]

    acc_ref[...] = jnp.zeros_like(acc_ref)
    nr_ref[...] = jnp.zeros_like(nr_ref)
    t_loc = lax.broadcasted_iota(jnp.int32, (tq, tk), 0)
    s_loc = lax.broadcasted_iota(jnp.int32, (tq, tk), 1)
    logits(0, 0)
    for j in range(per_q):
        logits(j + 1, (j + 1) % 2)
        probs(j % 2, ((per_q - 1 - j) * tk + s_loc) < t_loc)
        if j > 0:
            accumulate(j - 1, (j - 1) % 2)

    def pair(it, carry):
        j = per_q + 2 * it
        logits(j + 1, 1)
        probs(0, None)
        accumulate(j - 1, 1)
        logits(j + 2, 0)
        probs(1, None)
        accumulate(j, 0)
        return carry

    lax.fori_loop(0, qi * (per_q // 2), pair, 0)
    accumulate(n_blocks - 1, 1)
    o_ref[0] = acc_ref[...].astype(BF16)


def _stickbreak(u3):
    b, s, _ = u3.shape
    tq, tk = SB_QUERY_TILE, SB_KEY_TILE
    return pl.pallas_call(
        _stickbreak_body,
        grid=(b, s // tq),
        in_specs=[pl.BlockSpec((1, tq, D_BRANCH), lambda bi, i: (bi, i, COL_SB_Q)),
                  pl.BlockSpec((1, s, D_BRANCH), lambda bi, i: (bi, 0, COL_SB_K)),
                  pl.BlockSpec((1, s, D_BRANCH), lambda bi, i: (bi, 0, COL_SB_V))],
        out_specs=pl.BlockSpec((1, tq, D_BRANCH), lambda bi, i: (bi, i, 0)),
        out_shape=jax.ShapeDtypeStruct((b, s, D_BRANCH), BF16),
        scratch_shapes=[pltpu.VMEM((s // tk, D_BRANCH, N_HEADS * tk), BF16),
                        pltpu.VMEM((s // tk, N_HEADS * tk, D_BRANCH), BF16),
                        pltpu.VMEM((2, tq, N_HEADS * tk), F32),
                        pltpu.VMEM((2, tq, N_HEADS * tk), BF16),
                        pltpu.VMEM((2, tq, D_BRANCH), F32),
                        pltpu.VMEM((tq, D_BRANCH), F32),
                        pltpu.VMEM((tq, D_BRANCH), F32)],
        compiler_params=_params("arbitrary", "arbitrary"),
        name="stickbreak",
    )(u3, u3, u3)


def _rope_body(pos_ref, f_ref, cos_ref, sin_ref):
    ang = pos_ref[0].astype(F32) * f_ref[...]
    cos_ref[0] = jnp.cos(ang)
    sin_ref[0] = jnp.sin(ang)


def _rope_tables(positions):
    b, s = positions.shape
    half = HEAD_DIM // 2
    inv_freq = ROPE_THETA ** (-jnp.arange(half, dtype=F32) / half)
    freq = jnp.tile(inv_freq, 128 // half)[None, :]
    ts = SEQ_TILE
    out = jax.ShapeDtypeStruct((b, s, 128), F32)
    return pl.pallas_call(
        _rope_body,
        grid=(b, s // ts),
        in_specs=[pl.BlockSpec((1, ts, 1), lambda bi, i: (bi, i, 0)), _resident((1, 128))],
        out_specs=[pl.BlockSpec((1, ts, 128), lambda bi, i: (bi, i, 0))] * 2,
        out_shape=[out, out],
        compiler_params=_params("parallel", "parallel"),
        name="rope_tables",
    )(positions[:, :, None], freq)


def _retention_body(q_ref, k_ref, v_ref, g_ref, cos_ref, sin_ref, dec_ref, xi_ref, zeta_ref,
                    gam_ref, o_ref, state_ref):
    @pl.when(pl.program_id(1) == 0)
    def _():
        state_ref[...] = jnp.zeros_like(state_ref)

    lane = lax.broadcasted_iota(jnp.int32, (1, D_BRANCH), 1)
    lane_head = lax.shift_right_logical(lane, 6)
    first_half = (lane & (HEAD_DIM - 1)) < HEAD_DIM // 2
    r = lax.broadcasted_iota(jnp.int32, (D_BRANCH, D_BRANCH), 0)
    c = lax.broadcasted_iota(jnp.int32, (D_BRANCH, D_BRANCH), 1)
    same_head = lax.shift_right_logical(r, 6) == lax.shift_right_logical(c, 6)
    gm = _group_mean_matrix()

    for n in range(q_ref.shape[0]):
        cos = jnp.concatenate([cos_ref[n], cos_ref[n]], axis=1)
        sin = jnp.concatenate([sin_ref[n], sin_ref[n]], axis=1)

        def rope(x):
            partner = jnp.where(first_half, -pltpu.roll(x, D_BRANCH - HEAD_DIM // 2, 1),
                                pltpu.roll(x, HEAD_DIM // 2, 1))
            return x * cos + partner * sin

        q = rope(q_ref[n].astype(F32))
        k = rope(k_ref[n].astype(F32)) * HEAD_DIM ** -0.5
        v = v_ref[n]
        kexp = _head_expand(k.astype(BF16), lane_head)
        vexp = _head_expand(v, lane_head)
        scores = _dot_nt(q.astype(BF16), kexp) * dec_ref[...]
        inner = _dot(scores.astype(BF16), vexp)
        state = state_ref[n]
        cross = _dot((q * xi_ref[...]).astype(BF16), state.astype(BF16))
        update = _dot_tn((k * zeta_ref[...]).astype(BF16), v)
        state_ref[n] = state * gam_ref[...] + jnp.where(same_head, update, 0.0)

        o = inner + cross
        oc = o - _split_dot(o, gm)
        var = _split_dot(oc * oc, gm)
        o_ref[n] = (_silu(g_ref[n].astype(F32)) * (oc * lax.rsqrt(var + EPS))).astype(BF16)


def _retention(u3, cos_t, sin_t):
    b, s, _ = u3.shape
    tc = SEQ_TILE
    gammas = 1.0 - jnp.exp2(-5.0 - jnp.arange(N_HEADS, dtype=F32))
    log_g = jnp.log(gammas)
    log_g_lane = jnp.repeat(log_g, HEAD_DIM)[None, :]
    idx = jnp.arange(tc, dtype=F32)
    rel = idx[:, None] - idx[None, :]
    decay = jnp.where(rel >= 0, jnp.exp(jnp.maximum(rel, 0.0) * log_g[:, None, None]), 0.0)
    decay = jnp.transpose(decay, (1, 0, 2)).reshape(tc, N_HEADS * tc)
    xi = jnp.exp((idx + 1.0)[:, None] * log_g_lane)
    zeta = jnp.exp((tc - 1.0 - idx)[:, None] * log_g_lane)
    gam = jnp.broadcast_to(jnp.exp(tc * log_g_lane).T, (D_BRANCH, D_BRANCH))

    nb = math.gcd(b, RET_BATCH)

    def col(cb):
        return pl.BlockSpec((nb, tc, D_BRANCH), lambda bi, i: (bi, i, cb))

    tab = pl.BlockSpec((nb, tc, 128), lambda bi, i: (bi, i, 0))
    return pl.pallas_call(
        _retention_body,
        grid=(b // nb, s // tc),
        in_specs=[col(COL_RET_Q), col(COL_RET_K), col(COL_RET_V), col(COL_RET_G), tab, tab,
                  _resident((tc, N_HEADS * tc)), _resident((tc, D_BRANCH)),
                  _resident((tc, D_BRANCH)), _resident((D_BRANCH, D_BRANCH))],
        out_specs=pl.BlockSpec((nb, tc, D_BRANCH), lambda bi, i: (bi, i, 0)),
        out_shape=jax.ShapeDtypeStruct((b, s, D_BRANCH), BF16),
        scratch_shapes=[pltpu.VMEM((nb, D_BRANCH, D_BRANCH), F32)],
        compiler_params=_params("parallel", "arbitrary"),
        name="retention",
    )(u3, u3, u3, u3, cos_t, sin_t, decay, xi, zeta, gam)


def _merge_body(x_ref, ycp_ref, ysb_ref, yret_ref, g_ref, wg_ref, wb_ref, wo_ref, o_ref):
    x = x_ref[...]
    h = _rms_rows(x, g_ref[...]).astype(BF16)
    ycp = ycp_ref[...]
    branches = (ycp[:, :D_BRANCH], ycp[:, D_BRANCH:], ysb_ref[...], yret_ref[...])
    parts = []
    for n in range(D_MODEL // D_BRANCH):
        cols = slice(n * D_BRANCH, (n + 1) * D_BRANCH)
        m = None
        for i, y in enumerate(branches):
            term = _sigmoid(_dot(h, wg_ref[i, :, cols])) * _dot(y, wb_ref[i, :, cols])
            m = term if m is None else m + term
        parts.append(m.astype(BF16))
    o_ref[...] = x + _dot(jnp.concatenate(parts, axis=1), wo_ref[...])


def _merge(x2, ycp, ysb, yret, g, wg, wb, wo):
    t = x2.shape[0]
    tm = TOKEN_TILE

    def rows(width):
        return pl.BlockSpec((tm, width), lambda i: (i, 0))

    return pl.pallas_call(
        _merge_body,
        grid=(t // tm,),
        in_specs=[rows(D_MODEL), rows(2 * D_BRANCH), rows(D_BRANCH), rows(D_BRANCH),
                  _resident((1, D_MODEL)), _resident((N_HEADS, D_MODEL, D_MODEL)),
                  _resident((N_HEADS, D_BRANCH, D_MODEL)), _resident((D_MODEL, D_MODEL))],
        out_specs=rows(D_MODEL),
        out_shape=jax.ShapeDtypeStruct((t, D_MODEL), F32),
        compiler_params=_params("parallel"),
        name="merge",
    )(x2, ycp, ysb, yret, g, wg, wb, wo)


def _head_rms(x, gain_lanes, gm):
    ms = _split_dot(x * x, gm)
    return x * lax.rsqrt(ms + EPS) * gain_lanes


def _mem_kv_body(m_ref, g_ref, w_ref, gk_ref, k_ref, v_ref):
    hm = _rms_rows(m_ref[0], g_ref[...]).astype(BF16)
    kv = _dot(hm, w_ref[...])
    k_ref[0] = _head_rms(kv[:, :D_BRANCH], gk_ref[...], _group_mean_matrix()).astype(BF16)
    v_ref[0] = kv[:, D_BRANCH:].astype(BF16)


def _mem_kv(mem, g, wkv, gk_lanes):
    b, m, _ = mem.shape
    out = jax.ShapeDtypeStruct((b, m, D_BRANCH), BF16)
    blk = pl.BlockSpec((1, m, D_BRANCH), lambda bi: (bi, 0, 0))
    return pl.pallas_call(
        _mem_kv_body,
        grid=(b,),
        in_specs=[pl.BlockSpec((1, m, D_MODEL), lambda bi: (bi, 0, 0)), _resident((1, D_MODEL)),
                  _resident((D_MODEL, 2 * D_BRANCH)), _resident((1, D_BRANCH))],
        out_specs=[blk, blk],
        out_shape=[out, out],
        compiler_params=_params("parallel"),
        name="mem_kv",
    )(mem, g, wkv, gk_lanes)


def _xattn_body(x_ref, k_ref, v_ref, g_ref, wq_ref, gq_ref, wo_ref, o_ref):
    x = x_ref[0]
    m = k_ref.shape[1]
    h = _rms_rows(x, g_ref[...]).astype(BF16)
    q = _head_rms(_dot(h, wq_ref[...]), gq_ref[...], _group_mean_matrix()).astype(BF16)
    lane_head = _lane_head()
    s_all = _dot_nt(q, _head_expand(k_ref[0], lane_head)) * HEAD_DIM ** -0.5
    probs = []
    for hd in range(N_HEADS):
        s = s_all[:, hd * m:(hd + 1) * m]
        e = jnp.exp(s - jnp.max(s, axis=-1, keepdims=True))
        probs.append((e / jnp.sum(e, axis=-1, keepdims=True)).astype(BF16))
    o = _dot(jnp.concatenate(probs, axis=1), _head_expand(v_ref[0], lane_head))
    o_ref[0] = x + _dot(o.astype(BF16), wo_ref[...])


def _xattn(x3, k, v, g, wq, gq_lanes, wo):
    b, s, _ = x3.shape
    m = k.shape[1]
    ts = TOKEN_TILE
    xblk = pl.BlockSpec((1, ts, D_MODEL), lambda bi, i: (bi, i, 0))
    kvblk = pl.BlockSpec((1, m, D_BRANCH), lambda bi, i: (bi, 0, 0))
    return pl.pallas_call(
        _xattn_body,
        grid=(b, s // ts),
        in_specs=[xblk, kvblk, kvblk, _resident((1, D_MODEL)), _resident((D_MODEL, D_BRANCH)),
                  _resident((1, D_BRANCH)), _resident((D_BRANCH, D_MODEL))],
        out_specs=xblk,
        out_shape=jax.ShapeDtypeStruct((b, s, D_MODEL), F32),
        compiler_params=_params("parallel", "parallel"),
        name="xattn",
    )(x3, k, v, g, wq, gq_lanes, wo)


def _swiglu_rows(h, wgu_ref, wd_ref, acc_ref, between=None):
    for j in range(D_FF // FF_CHUNK):
        cols = slice(j * FF_CHUNK, (j + 1) * FF_CHUNK)
        up_cols = slice(D_FF + j * FF_CHUNK, D_FF + (j + 1) * FF_CHUNK)
        a = (_silu(_dot(h, wgu_ref[:, cols])) * _dot(h, wgu_ref[:, up_cols])).astype(BF16)
        part = _dot(a, wd_ref[cols, :])
        if j == 0:
            acc_ref[...] = part
        else:
            acc_ref[...] += part
        if between is not None:
            between(j)


def _ffn_body(x_ref, g_ref, wgu_ref, wd_ref, o_ref, acc_ref):
    x = x_ref[...]
    h = _rms_rows(x, g_ref[...]).astype(BF16)
    _swiglu_rows(h, wgu_ref, wd_ref, acc_ref)
    o_ref[...] = x + acc_ref[...]


def _ffn(x2, g, wgu, wd):
    t = x2.shape[0]
    tm = TOKEN_TILE
    rows = pl.BlockSpec((tm, D_MODEL), lambda i: (i, 0))
    return pl.pallas_call(
        _ffn_body,
        grid=(t // tm,),
        in_specs=[rows, _resident((1, D_MODEL)), _resident((D_MODEL, 2 * D_FF)),
                  _resident((D_FF, D_MODEL))],
        out_specs=rows,
        out_shape=jax.ShapeDtypeStruct((t, D_MODEL), F32),
        scratch_shapes=[pltpu.VMEM((tm, D_MODEL), F32)],
        compiler_params=_params("parallel"),
        name="ffn",
    )(x2, g, wgu, wd)


LANE_TILES = D_MODEL // 128


def _to_token_tiles(x, o_ref):
    n = x.shape[0]
    for c in range(LANE_TILES):
        o_ref[pl.ds(c, n, stride=LANE_TILES), :] = x[:, c * 128:(c + 1) * 128]


def _from_token_tiles(ref, n):
    return jnp.concatenate([ref[pl.ds(c, n, stride=LANE_TILES), :] for c in range(LANE_TILES)], axis=1)


def _router_body(x_ref, g_ref, rhi_ref, rlo_ref, o_ref, xt_ref):
    x = x_ref[...]
    _to_token_tiles(x, xt_ref)
    h = _rms_rows(x, g_ref[...])
    hi = h.astype(BF16)
    lo = (h - hi.astype(F32)).astype(BF16)
    logits = _dot(hi, rhi_ref[...]) + (_dot(hi, rlo_ref[...]) + _dot(lo, rhi_ref[...]))
    lane = lax.broadcasted_iota(jnp.int32, logits.shape, 1)
    neg = jnp.float32(-jnp.inf)
    logits = jnp.where(lane < N_EXPERTS, logits, neg)
    m1 = jnp.max(logits, axis=-1, keepdims=True)
    i1 = jnp.min(jnp.where(logits == m1, lane, ROUTER_LANES), axis=-1, keepdims=True)
    rest = jnp.where(lane == i1, neg, logits)
    m2 = jnp.max(rest, axis=-1, keepdims=True)
    i2 = jnp.min(jnp.where(rest == m2, lane, ROUTER_LANES), axis=-1, keepdims=True)
    e2 = jnp.exp(m2 - m1)
    w1 = 1.0 / (1.0 + e2)
    w2 = e2 / (1.0 + e2)
    out = jnp.where(lane == 0, i1.astype(F32), 0.0)
    out = jnp.where(lane == 1, i2.astype(F32), out)
    out = jnp.where(lane == 2, w1, out)
    o_ref[...] = jnp.where(lane == 3, w2, out)


def _router(x2, g, r_hi, r_lo):
    t = x2.shape[0]
    tm = TOKEN_TILE
    return pl.pallas_call(
        _router_body,
        grid=(t // tm,),
        in_specs=[pl.BlockSpec((tm, D_MODEL), lambda i: (i, 0)), _resident((1, D_MODEL)),
                  _resident((D_MODEL, ROUTER_LANES)), _resident((D_MODEL, ROUTER_LANES))],
        out_specs=[pl.BlockSpec((tm, ROUTER_LANES), lambda i: (i, 0)),
                   pl.BlockSpec((tm * LANE_TILES, 128), lambda i: (i, 0))],
        out_shape=[jax.ShapeDtypeStruct((t, ROUTER_LANES), F32),
                   jax.ShapeDtypeStruct((t * LANE_TILES, 128), F32)],
        compiler_params=_params("parallel"),
        name="router",
    )(x2, g, r_hi, r_lo)


def _experts_body(tile_expert_ref, n_tiles_ref, src_ref, src_next_ref, dst_prev_ref, dst_ref,
                  x_hbm, g_ref, wgu_ref, wd_ref, y_hbm, xbuf_ref, ybuf_ref, acc_ref, gsem, ssem):
    i = pl.program_id(0)
    last = pl.num_programs(0) - 1
    slot = lax.rem(i, 2)
    other = 1 - slot
    tm = EXPERT_TILE
    rows = LANE_TILES

    def gather(idx_ref, r, s):
        src = pl.multiple_of(idx_ref[0, 0, r] * rows, rows)
        return pltpu.make_async_copy(x_hbm.at[pl.ds(src, rows)],
                                     xbuf_ref.at[s, pl.ds(r * rows, rows)], gsem)

    def scatter(idx_ref, r, s):
        dst = pl.multiple_of(idx_ref[0, 0, r] * rows, rows)
        return pltpu.make_async_copy(ybuf_ref.at[s, pl.ds(r * rows, rows)],
                                     y_hbm.at[pl.ds(dst, rows)], ssem)

    def wait_gather():
        pltpu.make_async_copy(x_hbm.at[pl.ds(0, tm * rows)], xbuf_ref.at[0], gsem).wait()

    def wait_scatter():
        pltpu.make_async_copy(ybuf_ref.at[0], y_hbm.at[pl.ds(0, tm * rows)], ssem).wait()

    def start_next(r, lane=0):
        gather(src_next_ref, r, other).start(priority=lane)
        scatter(dst_prev_ref, r, other).start(priority=1 - lane)

    @pl.when(i == 0)
    def _():
        ybuf_ref[...] = jnp.zeros_like(ybuf_ref)
        lax.fori_loop(0, tm, lambda r, c: (gather(src_ref, r, 0).start(), c)[1], 0)

    wait_gather()

    @pl.when(i > 0)
    def _():
        wait_scatter()

    @pl.when(i < n_tiles_ref[0])
    def _():
        h = _rms_rows(_from_token_tiles(xbuf_ref.at[slot], tm), g_ref[...]).astype(BF16)
        per = -(-tm // EXPERT_ISSUE_CHUNKS)

        def between(j):
            for r in range(j * per, min((j + 1) * per, tm)):
                start_next(r, r % 2)

        _swiglu_rows(h, wgu_ref.at[0], wd_ref.at[0], acc_ref, between)
        _to_token_tiles(acc_ref[...], ybuf_ref.at[slot])

    @pl.when(i >= n_tiles_ref[0])
    def _():
        ybuf_ref[slot] = jnp.zeros((tm * rows, 128), F32)
        lax.fori_loop(0, tm, lambda r, c: (start_next(r), c)[1], 0)

    @pl.when(i == last)
    def _():
        wait_gather()
        wait_scatter()
        lax.fori_loop(0, tm, lambda r, c: (scatter(dst_ref, r, slot).start(), c)[1], 0)
        wait_scatter()


def _experts(x_tiles, src_tiles, dst_tiles, tile_expert, n_tiles, g, wgu, wd):
    n_steps = src_tiles.shape[0] - 1
    tm = EXPERT_TILE

    def idx(off):
        return pl.BlockSpec((1, 1, tm), lambda i, te, nt: (i + off, 0, 0), memory_space=pltpu.SMEM)

    def expert(shape):
        return pl.BlockSpec((1,) + shape, lambda i, te, nt: (te[i], 0, 0),
                            pipeline_mode=pl.Buffered(1))

    grid_spec = pltpu.PrefetchScalarGridSpec(
        num_scalar_prefetch=2,
        grid=(n_steps,),
        in_specs=[idx(0), idx(1), idx(0), idx(1),
                  pl.BlockSpec(memory_space=pl.ANY),
                  pl.BlockSpec((1, D_MODEL), lambda i, te, nt: (0, 0)),
                  expert((D_MODEL, 2 * D_FF)), expert((D_FF, D_MODEL))],
        out_specs=pl.BlockSpec(memory_space=pl.ANY),
        scratch_shapes=[pltpu.VMEM((2, tm * LANE_TILES, 128), F32),
                        pltpu.VMEM((2, tm * LANE_TILES, 128), F32),
                        pltpu.VMEM((tm, D_MODEL), F32),
                        pltpu.SemaphoreType.DMA(()), pltpu.SemaphoreType.DMA(())],
    )
    return pl.pallas_call(
        _experts_body,
        grid_spec=grid_spec,
        out_shape=jax.ShapeDtypeStruct(((n_steps + 1) * tm * LANE_TILES, 128), F32),
        compiler_params=_params("arbitrary"),
        name="experts",
    )(tile_expert, n_tiles, src_tiles, src_tiles, dst_tiles, dst_tiles, x_tiles, g, wgu, wd)


def _combine_body(x_ref, y0_ref, y1_ref, r_ref, o_ref):
    routed = r_ref[...]
    tm = x_ref.shape[0]
    y0 = _from_token_tiles(y0_ref, tm)
    y1 = _from_token_tiles(y1_ref, tm)
    o_ref[...] = x_ref[...] + (routed[:, 2:3] * y0 + routed[:, 3:4] * y1)


def _combine(x2, y_tiles, routed):
    t = x2.shape[0]
    tm = TOKEN_TILE
    rows = pl.BlockSpec((tm, D_MODEL), lambda i: (i, 0))
    return pl.pallas_call(
        _combine_body,
        grid=(t // tm,),
        in_specs=[rows, pl.BlockSpec((tm * LANE_TILES, 128), lambda i: (i, 0)),
                  pl.BlockSpec((tm * LANE_TILES, 128), lambda i: (i + t // tm, 0)),
                  pl.BlockSpec((tm, ROUTER_LANES), lambda i: (i, 0))],
        out_specs=rows,
        out_shape=jax.ShapeDtypeStruct((t, D_MODEL), F32),
        compiler_params=_params("parallel"),
        name="combine",
    )(x2, y_tiles, y_tiles, routed)


def _moe(x2, g, router, wgu, wd):
    t = x2.shape[0]
    tm = EXPERT_TILE
    r_pad = jnp.pad(router, ((0, 0), (0, ROUTER_LANES - N_EXPERTS)))
    r_hi = r_pad.astype(BF16)
    r_lo = (r_pad - r_hi.astype(F32)).astype(BF16)
    routed, x_tiles = _router(x2, g, r_hi, r_lo)
    expert_of = routed[:, 0:2].astype(jnp.int32).T.reshape(-1)

    n_steps = (2 * t) // tm + N_EXPERTS
    n_pad = n_steps * tm - 2 * t
    counts = jnp.sum((expert_of[:, None] == jnp.arange(N_EXPERTS, dtype=jnp.int32)[None, :])
                     .astype(jnp.int32), axis=0)
    padded = ((counts + tm - 1) // tm) * tm
    pad_ends = jnp.cumsum(padded - counts)
    filler = jnp.arange(n_pad, dtype=jnp.int32)
    filler_expert = jnp.sum((filler[:, None] >= pad_ends[None, :]).astype(jnp.int32), axis=1)
    shift = 1 + (2 * t - 1).bit_length()
    keys = jnp.concatenate([expert_of * (1 << shift) + jnp.arange(2 * t, dtype=jnp.int32),
                            filler_expert * (1 << shift) + (1 << (shift - 1)) + filler])
    vals = jnp.concatenate([jnp.arange(2 * t, dtype=jnp.int32), 2 * t + tm + filler])
    _, dst = lax.sort((keys, vals), num_keys=1)
    src = jnp.where(dst < 2 * t, dst % t, 0)
    spare_tile = 2 * t + jnp.arange(tm, dtype=jnp.int32)
    dst_tiles = jnp.concatenate([spare_tile, dst]).reshape(n_steps + 1, 1, tm)
    src_tiles = jnp.concatenate([src, jnp.zeros((tm,), jnp.int32)]).reshape(n_steps + 1, 1, tm)
    ends = jnp.cumsum(padded)
    n_tiles = (ends[-1] // tm).astype(jnp.int32).reshape(1)
    tile_start = jnp.minimum(jnp.arange(n_steps, dtype=jnp.int32), n_tiles[0] - 1) * tm
    tile_expert = jnp.sum((tile_start[:, None] >= ends[None, :]).astype(jnp.int32), axis=1)

    y_tiles = _experts(x_tiles, src_tiles, dst_tiles, tile_expert, n_tiles, g, wgu, wd)
    return _combine(x2, y_tiles, routed)


def _block_diag(w):
    g, n, _ = w.shape
    eye = jnp.eye(g, dtype=w.dtype)
    return (eye[:, None, :, None] * w[:, :, None, :]).reshape(g * n, g * n)


def kernel(x, mem, positions, norm_mix_g, w_in, conv_w, conv_b, conv_ln_g, conv_ln_b, pool_w, pool_scale, w_gate, w_branch, w_out, norm_xa_g, norm_mem_g, xa_wq, xa_wkv, xa_q_norm_g, xa_k_norm_g, xa_wo, norm_ffn_g, ffn_w_gu, ffn_w_down, moe_router, moe_w_gu, moe_w_down):
    b, s, d = x.shape
    depth = w_in.shape[0]
    t = b * s
    assert d == D_MODEL and s % TOKEN_TILE == 0 and s % SEQ_TILE == 0

    def row(v):
        return v.astype(F32)[None, :]

    cos_t, sin_t = _rope_tables(positions)
    x2 = x.astype(F32).reshape(t, d)
    for l in range(depth):
        cw = jnp.pad(conv_w[l].astype(F32), ((0, SEQ_HALO - CONV_WIDTH), (0, 0)))
        ycp, u_att = _mix_front(x2.reshape(b, s, d), row(norm_mix_g[l]), w_in[l].astype(BF16), cw,
                                row(conv_b[l]), row(conv_ln_g[l]), row(conv_ln_b[l]),
                                _block_diag(pool_w[l]).astype(BF16), row(pool_scale[l]))
        ysb = _stickbreak(u_att)
        yret = _retention(u_att, cos_t, sin_t)
        x2 = _merge(x2, ycp.reshape(t, 2 * D_BRANCH), ysb.reshape(t, D_BRANCH),
                    yret.reshape(t, D_BRANCH), row(norm_mix_g[l]), w_gate[l].astype(BF16),
                    w_branch[l].astype(BF16), w_out[l].astype(BF16))
        k_mem, v_mem = _mem_kv(mem.astype(F32), row(norm_mem_g[l]), xa_wkv[l].astype(BF16),
                               row(jnp.tile(xa_k_norm_g[l], N_HEADS)))
        x2 = _xattn(x2.reshape(b, s, d), k_mem, v_mem, row(norm_xa_g[l]), xa_wq[l].astype(BF16),
                    row(jnp.tile(xa_q_norm_g[l], N_HEADS)), xa_wo[l].astype(BF16)).reshape(t, d)
        g_ffn = row(norm_ffn_g[l])
        if l % 2 == 0:
            x2 = _ffn(x2, g_ffn, ffn_w_gu[l // 2].astype(BF16), ffn_w_down[l // 2].astype(BF16))
        else:
            x2 = _moe(x2, g_ffn, moe_router[l // 2].astype(F32), moe_w_gu[l // 2].astype(BF16),
                      moe_w_down[l // 2].astype(BF16))
    return x2.reshape(b, s, d).astype(x.dtype)
```

```python
import math

import jax
import jax.numpy as jnp
from jax import lax
from jax.experimental import pallas as pl
from jax.experimental.pallas import tpu as pltpu

F32 = jnp.float32
BF16 = jnp.bfloat16

D_MODEL = 1024
HEAD_DIM = 64
N_HEADS = 4
D_BRANCH = N_HEADS * HEAD_DIM
CONV_WIDTH = 31
POOL_WINDOWS = (2, 4, 8, 16)
D_FF = 2816
N_EXPERTS = 8
ROPE_THETA = 10000.0
EPS = 1e-6

COL_CONV = 0
COL_POOL = 2
D_IN = 10 * D_BRANCH
COL_SB_Q, COL_SB_K, COL_SB_V = 0, 1, 2
COL_RET_Q, COL_RET_K, COL_RET_V, COL_RET_G = 3, 4, 5, 6
D_ATT = 7 * D_BRANCH

V7X_VMEM_BYTES = 64 * 1024 * 1024
VMEM_LIMIT_BYTES = V7X_VMEM_BYTES - 8 * 1024 * 1024

TOKEN_TILE = 512
SEQ_TILE = 256
SB_QUERY_TILE = 512
SB_KEY_TILE = 256
SB_CLAMP = 30.0
LOG2E = 1.4426950408889634
RET_BATCH = 4
XATTN_BATCH = 2
SEQ_HALO = 32
CONV_ROWS = 64
FF_CHUNK = 256
EXPERT_TILE = 512
EXPERT_ISSUE_CHUNKS = 6
ROUTER_LANES = 128


def _resident(shape):
    return pl.BlockSpec(shape, lambda *_: (0,) * len(shape), pipeline_mode=pl.Buffered(1))


def _params(*semantics):
    return pltpu.CompilerParams(dimension_semantics=semantics, vmem_limit_bytes=VMEM_LIMIT_BYTES)


def _sigmoid(x):
    return 0.5 * jnp.tanh(0.5 * x) + 0.5


def _silu(x):
    return x * _sigmoid(x)


def _rms_rows(x, g):
    ms = jnp.mean(x * x, axis=-1, keepdims=True)
    return x * lax.rsqrt(ms + EPS) * g


def _dot(a, b):
    return jnp.dot(a, b, preferred_element_type=F32)


def _dot_nt(a, b):
    return lax.dot_general(a, b, (((1,), (1,)), ((), ())), preferred_element_type=F32)


def _dot_tn(a, b):
    return lax.dot_general(a, b, (((0,), (0,)), ((), ())), preferred_element_type=F32)


def _split_dot(x, w):
    hi = x.astype(BF16)
    lo = (x - hi.astype(F32)).astype(BF16)
    return _dot(hi, w) + _dot(lo, w)


def _head_expand(x, lane_head):
    zero = jnp.zeros_like(x)
    return jnp.concatenate([jnp.where(lane_head == h, x, zero) for h in range(N_HEADS)], axis=0)


def _lane_head(width=D_BRANCH):
    return lax.shift_right_logical(lax.broadcasted_iota(jnp.int32, (1, width), 1), 6)


def _group_mean_matrix():
    r = lax.broadcasted_iota(jnp.int32, (D_BRANCH, D_BRANCH), 0)
    c = lax.broadcasted_iota(jnp.int32, (D_BRANCH, D_BRANCH), 1)
    same = lax.shift_right_logical(r, 6) == lax.shift_right_logical(c, 6)
    return jnp.where(same, 1.0 / HEAD_DIM, 0.0).astype(BF16)


def _mix_front_body(x_ref, g_ref, w_ref, cw_ref, cb_ref, lg_ref, lb_ref, pw_ref, ps_ref,
                    y_ref, ua_ref, vs_ref, sh_ref, co_ref, pp_ref, q_ref):
    ts = x_ref.shape[1]
    i = pl.program_id(1)
    h = _rms_rows(x_ref[0], g_ref[...]).astype(BF16)
    att = (COL_POOL + 1) * D_BRANCH
    u = _dot(h, w_ref[:, 0:att])
    ua = _dot(h, w_ref[:, att:])
    ua_ref[0, :, 0:D_BRANCH] = (ua[:, 0:D_BRANCH] * (LOG2E * HEAD_DIM ** -0.5)).astype(BF16)
    ua_ref[0, :, D_BRANCH:] = ua[:, D_BRANCH:].astype(BF16)

    @pl.when(i == 0)
    def _():
        vs_ref[0:SEQ_HALO, :] = jnp.zeros((SEQ_HALO, D_BRANCH), F32)
        pp_ref[0:SEQ_HALO, :] = jnp.zeros((SEQ_HALO, D_BRANCH), F32)

    vs_ref[SEQ_HALO:SEQ_HALO + ts, :] = u[:, 0:D_BRANCH] * _sigmoid(u[:, D_BRANCH:2 * D_BRANCH])
    span = ts + SEQ_HALO - 8
    for r in range(1, 8):
        sh_ref[r, 0:span, :] = vs_ref[r:r + span, :]
    lead = SEQ_HALO - (CONV_WIDTH - 1)
    for r0 in range(0, ts, CONV_ROWS):
        acc = jnp.zeros((CONV_ROWS, D_BRANCH), F32)
        for k in range(CONV_WIDTH):
            a, r = divmod(lead + k, 8)
            rows = slice(r0 + 8 * a, r0 + 8 * a + CONV_ROWS)
            tap = vs_ref[rows, :] if r == 0 else sh_ref[r, rows, :]
            acc = acc + cw_ref[k:k + 1, :] * tap
        co_ref[r0:r0 + CONV_ROWS, :] = acc
    c = co_ref[...] + cb_ref[...]
    mu = jnp.mean(c, axis=-1, keepdims=True)
    cc = c - mu
    var = jnp.mean(cc * cc, axis=-1, keepdims=True)
    y_ref[0, :, 0:D_BRANCH] = _silu(cc * lax.rsqrt(var + EPS) * lg_ref[...] + lb_ref[...]).astype(BF16)
    vs_ref[0:SEQ_HALO, :] = vs_ref[ts:ts + SEQ_HALO, :]

    up = u[:, COL_POOL * D_BRANCH:att]
    pp_ref[SEQ_HALO:SEQ_HALO + ts, :] = up
    end = SEQ_HALO + ts
    t_pos = (i * ts + lax.broadcasted_iota(jnp.int32, (ts, 1), 0)).astype(F32) + 1.0
    lane = lax.broadcasted_iota(jnp.int32, (1, 128), 1)
    means = []
    for half in range(2):
        cols = slice(half * 128, (half + 1) * 128)
        q_ref[0, 8:end, cols] = pp_ref[8:end, cols] + pp_ref[7:end - 1, cols]
        q_ref[1, 16:end, cols] = q_ref[0, 16:end, cols] + q_ref[0, 14:end - 2, cols]
        if half == 0:
            small, big = q_ref[0, SEQ_HALO:end, cols], q_ref[1, SEQ_HALO:end, cols]
        else:
            q_ref[2, 24:end, cols] = q_ref[1, 24:end, cols] + q_ref[1, 20:end - 4, cols]
            small = q_ref[2, SEQ_HALO:end, cols]
            big = small + q_ref[2, SEQ_HALO - 8:end - 8, cols]
        w_small, w_big = POOL_WINDOWS[2 * half], POOL_WINDOWS[2 * half + 1]
        m_small = small / jnp.minimum(t_pos, float(w_small))
        m_big = big / jnp.minimum(t_pos, float(w_big))
        means.append(jnp.where(lane < HEAD_DIM, m_small, m_big))
    d = (jnp.concatenate(means, axis=1) - up).astype(BF16)
    y_ref[0, :, D_BRANCH:2 * D_BRANCH] = (_dot(d, pw_ref[...]) * ps_ref[...]).astype(BF16)
    pp_ref[0:SEQ_HALO, :] = pp_ref[ts:ts + SEQ_HALO, :]


def _mix_front(x3, g, w, conv_w, conv_b, ln_g, ln_b, pool_w_bd, pool_scale):
    b, s, _ = x3.shape
    ts = TOKEN_TILE
    return pl.pallas_call(
        _mix_front_body,
        grid=(b, s // ts),
        in_specs=[pl.BlockSpec((1, ts, D_MODEL), lambda bi, i: (bi, i, 0)),
                  _resident((1, D_MODEL)), _resident((D_MODEL, D_IN)),
                  _resident((SEQ_HALO, D_BRANCH)), _resident((1, D_BRANCH)),
                  _resident((1, D_BRANCH)), _resident((1, D_BRANCH)),
                  _resident((D_BRANCH, D_BRANCH)), _resident((1, D_BRANCH))],
        out_specs=[pl.BlockSpec((1, ts, 2 * D_BRANCH), lambda bi, i: (bi, i, 0)),
                   pl.BlockSpec((1, ts, D_ATT), lambda bi, i: (bi, i, 0))],
        out_shape=[jax.ShapeDtypeStruct((b, s, 2 * D_BRANCH), BF16),
                   jax.ShapeDtypeStruct((b, s, D_ATT), BF16)],
        scratch_shapes=[pltpu.VMEM((SEQ_HALO + ts, D_BRANCH), F32),
                        pltpu.VMEM((8, SEQ_HALO + ts, D_BRANCH), F32),
                        pltpu.VMEM((ts, D_BRANCH), F32),
                        pltpu.VMEM((SEQ_HALO + ts, D_BRANCH), F32),
                        pltpu.VMEM((3, SEQ_HALO + ts, D_BRANCH), F32)],
        compiler_params=_params("arbitrary", "arbitrary"),
        name="mix_front",
    )(x3, g, w, conv_w, conv_b, ln_g, ln_b, pool_w_bd, pool_scale)


def _stickbreak_body(q_ref, k_ref, v_ref, o_ref, kexp_ref, vexp_ref, w_ref, p_ref, f_ref, nr_ref,
                     acc_ref):
    tq = q_ref.shape[1]
    tk = SB_KEY_TILE
    n_kb = k_ref.shape[1] // tk
    per_q = tq // tk
    assert per_q % 2 == 0
    qi = pl.program_id(1)
    lane_head = _lane_head()

    @pl.when(qi == 0)
    def _():
        row_head = lax.shift_right_logical(lax.broadcasted_iota(jnp.int32, (D_BRANCH, 1), 0), 6)

        def expand(j, carry):
            start = pl.multiple_of(j * tk, tk)
            kt = k_ref[0, pl.ds(start, tk), :].astype(F32).T
            kexp_ref[j] = jnp.concatenate(
                [jnp.where(row_head == h, kt, 0.0) for h in range(N_HEADS)], axis=1).astype(BF16)
            vexp_ref[j] = _head_expand(v_ref[0, pl.ds(start, tk), :], lane_head)
            return carry

        lax.fori_loop(0, n_kb, expand, 0)

    q = q_ref[0]
    row = lax.broadcasted_iota(jnp.int32, (tk, tk), 0)
    col = lax.broadcasted_iota(jnp.int32, (tk, tk), 1)
    tri = (row >= col).astype(BF16)

    n_blocks = (qi + 1) * per_q

    def key_block(j):
        return jnp.maximum(n_blocks - 1 - j, 0)

    def logits(j, slot, r0=0):
        w_ref[slot, r0:, :] = _dot(q[r0:], kexp_ref[key_block(j)])

    def probs(slot, mask, r0=0):
        neg_run = nr_ref[r0:, :]
        f_ref[slot, r0:, :] = jnp.exp2(neg_run)
        total = None
        for h in range(N_HEADS):
            w = w_ref[slot, r0:, h * tk:(h + 1) * tk]
            sp = jnp.maximum(jnp.log2(1.0 + jnp.exp2(jnp.minimum(w, SB_CLAMP))), w)
            if mask is not None:
                sp = jnp.where(mask, sp, 0.0)
            rev = _dot(sp.astype(BF16), tri)
            p = jnp.exp2(w - rev)
            if mask is not None:
                p = jnp.where(mask, p, 0.0)
            p_ref[slot, r0:, h * tk:(h + 1) * tk] = p.astype(BF16)
            total = rev[:, 0:1] if total is None else jnp.where(lane_head == h, rev[:, 0:1], total)
        nr_ref[r0:, :] = neg_run - total

    def accumulate(j, slot, r0=0):
        acc_ref[r0:, :] += _dot(p_ref[slot, r0:, :], vexp_ref[key_block(j)]) * f_ref[slot, r0:, :]

    acc_ref[...] = jnp.zeros_like(acc_ref)
    nr_ref[...] = jnp.zeros_like(nr_ref)
    t_loc = lax.broadcasted_iota(jnp.int32, (tq, tk), 0)
    s_loc = lax.broadcasted_iota(jnp.int32, (tq, tk), 1)

    def first_row(j):
        return max(per_q - 1 - j, 0) * tk

    logits(0, 0, first_row(0))
    for j in range(per_q):
        logits(j + 1, (j + 1) % 2, first_row(j + 1))
        r0 = first_row(j)
        probs(j % 2, ((r0 + s_loc) < t_loc)[r0:], r0)
        if j > 0:
            accumulate(j - 1, (j - 1) % 2, first_row(j - 1))

    def pair(it, carry):
        j = per_q + 2 * it
        logits(j + 1, 1)
        probs(0, None)
        accumulate(j - 1, 1)
        logits(j + 2, 0)
        probs(1, None)
        accumulate(j, 0)
        return carry

    lax.fori_loop(0, qi * (per_q // 2), pair, 0)
    accumulate(n_blocks - 1, 1)
    o_ref[0] = acc_ref[...].astype(BF16)


def _stickbreak(u3):
    b, s, _ = u3.shape
    tq, tk = SB_QUERY_TILE, SB_KEY_TILE
    return pl.pallas_call(
        _stickbreak_body,
        grid=(b, s // tq),
        in_specs=[pl.BlockSpec((1, tq, D_BRANCH), lambda bi, i: (bi, i, COL_SB_Q)),
                  pl.BlockSpec((1, s, D_BRANCH), lambda bi, i: (bi, 0, COL_SB_K)),
                  pl.BlockSpec((1, s, D_BRANCH), lambda bi, i: (bi, 0, COL_SB_V))],
        out_specs=pl.BlockSpec((1, tq, D_BRANCH), lambda bi, i: (bi, i, 0)),
        out_shape=jax.ShapeDtypeStruct((b, s, D_BRANCH), BF16),
        scratch_shapes=[pltpu.VMEM((s // tk, D_BRANCH, N_HEADS * tk), BF16),
                        pltpu.VMEM((s // tk, N_HEADS * tk, D_BRANCH), BF16),
                        pltpu.VMEM((2, tq, N_HEADS * tk), F32),
                        pltpu.VMEM((2, tq, N_HEADS * tk), BF16),
                        pltpu.VMEM((2, tq, D_BRANCH), F32),
                        pltpu.VMEM((tq, D_BRANCH), F32),
                        pltpu.VMEM((tq, D_BRANCH), F32)],
        compiler_params=_params("arbitrary", "arbitrary"),
        name="stickbreak",
    )(u3, u3, u3)


def _rope_body(pos_ref, f_ref, cos_ref, sin_ref):
    ang = pos_ref[0].astype(F32) * f_ref[...]
    cos_ref[0] = jnp.cos(ang)
    sin_ref[0] = jnp.sin(ang)


def _rope_tables(positions):
    b, s = positions.shape
    half = HEAD_DIM // 2
    inv_freq = ROPE_THETA ** (-jnp.arange(half, dtype=F32) / half)
    freq = jnp.tile(inv_freq, 128 // half)[None, :]
    ts = SEQ_TILE
    out = jax.ShapeDtypeStruct((b, s, 128), F32)
    return pl.pallas_call(
        _rope_body,
        grid=(b, s // ts),
        in_specs=[pl.BlockSpec((1, ts, 1), lambda bi, i: (bi, i, 0)), _resident((1, 128))],
        out_specs=[pl.BlockSpec((1, ts, 128), lambda bi, i: (bi, i, 0))] * 2,
        out_shape=[out, out],
        compiler_params=_params("parallel", "parallel"),
        name="rope_tables",
    )(positions[:, :, None], freq)


def _retention_body(q_ref, k_ref, v_ref, g_ref, cos_ref, sin_ref, dec_ref, xi_ref, zeta_ref,
                    gam_ref, o_ref, state_ref):
    @pl.when(pl.program_id(1) == 0)
    def _():
        state_ref[...] = jnp.zeros_like(state_ref)

    lane = lax.broadcasted_iota(jnp.int32, (1, D_BRANCH), 1)
    lane_head = lax.shift_right_logical(lane, 6)
    first_half = (lane & (HEAD_DIM - 1)) < HEAD_DIM // 2
    r = lax.broadcasted_iota(jnp.int32, (D_BRANCH, D_BRANCH), 0)
    c = lax.broadcasted_iota(jnp.int32, (D_BRANCH, D_BRANCH), 1)
    same_head = lax.shift_right_logical(r, 6) == lax.shift_right_logical(c, 6)
    gm = _group_mean_matrix()

    for n in range(q_ref.shape[0]):
        cos = jnp.concatenate([cos_ref[n], cos_ref[n]], axis=1)
        sin = jnp.concatenate([sin_ref[n], sin_ref[n]], axis=1)

        def rope(x):
            partner = jnp.where(first_half, -pltpu.roll(x, D_BRANCH - HEAD_DIM // 2, 1),
                                pltpu.roll(x, HEAD_DIM // 2, 1))
            return x * cos + partner * sin

        q = rope(q_ref[n].astype(F32))
        k = rope(k_ref[n].astype(F32)) * HEAD_DIM ** -0.5
        v = v_ref[n]
        kexp = _head_expand(k.astype(BF16), lane_head)
        vexp = _head_expand(v, lane_head)
        scores = _dot_nt(q.astype(BF16), kexp) * dec_ref[...]
        inner = _dot(scores.astype(BF16), vexp)
        state = state_ref[n]
        cross = _dot((q * xi_ref[...]).astype(BF16), state.astype(BF16))
        update = _dot_tn((k * zeta_ref[...]).astype(BF16), v)
        state_ref[n] = state * gam_ref[...] + jnp.where(same_head, update, 0.0)

        o = inner + cross
        oc = o - _split_dot(o, gm)
        var = _split_dot(oc * oc, gm)
        o_ref[n] = (_silu(g_ref[n].astype(F32)) * (oc * lax.rsqrt(var + EPS))).astype(BF16)


def _retention(u3, cos_t, sin_t):
    b, s, _ = u3.shape
    tc = SEQ_TILE
    gammas = 1.0 - jnp.exp2(-5.0 - jnp.arange(N_HEADS, dtype=F32))
    log_g = jnp.log(gammas)
    log_g_lane = jnp.repeat(log_g, HEAD_DIM)[None, :]
    idx = jnp.arange(tc, dtype=F32)
    rel = idx[:, None] - idx[None, :]
    decay = jnp.where(rel >= 0, jnp.exp(jnp.maximum(rel, 0.0) * log_g[:, None, None]), 0.0)
    decay = jnp.transpose(decay, (1, 0, 2)).reshape(tc, N_HEADS * tc)
    xi = jnp.exp((idx + 1.0)[:, None] * log_g_lane)
    zeta = jnp.exp((tc - 1.0 - idx)[:, None] * log_g_lane)
    gam = jnp.broadcast_to(jnp.exp(tc * log_g_lane).T, (D_BRANCH, D_BRANCH))

    nb = math.gcd(b, RET_BATCH)

    def col(cb):
        return pl.BlockSpec((nb, tc, D_BRANCH), lambda bi, i: (bi, i, cb))

    tab = pl.BlockSpec((nb, tc, 128), lambda bi, i: (bi, i, 0))
    return pl.pallas_call(
        _retention_body,
        grid=(b // nb, s // tc),
        in_specs=[col(COL_RET_Q), col(COL_RET_K), col(COL_RET_V), col(COL_RET_G), tab, tab,
                  _resident((tc, N_HEADS * tc)), _resident((tc, D_BRANCH)),
                  _resident((tc, D_BRANCH)), _resident((D_BRANCH, D_BRANCH))],
        out_specs=pl.BlockSpec((nb, tc, D_BRANCH), lambda bi, i: (bi, i, 0)),
        out_shape=jax.ShapeDtypeStruct((b, s, D_BRANCH), BF16),
        scratch_shapes=[pltpu.VMEM((nb, D_BRANCH, D_BRANCH), F32)],
        compiler_params=_params("parallel", "arbitrary"),
        name="retention",
    )(u3, u3, u3, u3, cos_t, sin_t, decay, xi, zeta, gam)


def _merge_body(x_ref, ycp_ref, ysb_ref, yret_ref, g_ref, wg_ref, wb_ref, wo_ref, o_ref):
    x = x_ref[...]
    h = _rms_rows(x, g_ref[...]).astype(BF16)
    ycp = ycp_ref[...]
    branches = (ycp[:, :D_BRANCH], ycp[:, D_BRANCH:], ysb_ref[...], yret_ref[...])
    parts = []
    for n in range(D_MODEL // D_BRANCH):
        cols = slice(n * D_BRANCH, (n + 1) * D_BRANCH)
        m = None
        for i, y in enumerate(branches):
            term = _sigmoid(_dot(h, wg_ref[i, :, cols])) * _dot(y, wb_ref[i, :, cols])
            m = term if m is None else m + term
        parts.append(m.astype(BF16))
    o_ref[...] = x + _dot(jnp.concatenate(parts, axis=1), wo_ref[...])


def _merge(x2, ycp, ysb, yret, g, wg, wb, wo):
    t = x2.shape[0]
    tm = TOKEN_TILE

    def rows(width):
        return pl.BlockSpec((tm, width), lambda i: (i, 0))

    return pl.pallas_call(
        _merge_body,
        grid=(t // tm,),
        in_specs=[rows(D_MODEL), rows(2 * D_BRANCH), rows(D_BRANCH), rows(D_BRANCH),
                  _resident((1, D_MODEL)), _resident((N_HEADS, D_MODEL, D_MODEL)),
                  _resident((N_HEADS, D_BRANCH, D_MODEL)), _resident((D_MODEL, D_MODEL))],
        out_specs=rows(D_MODEL),
        out_shape=jax.ShapeDtypeStruct((t, D_MODEL), F32),
        compiler_params=_params("parallel"),
        name="merge",
    )(x2, ycp, ysb, yret, g, wg, wb, wo)


def _head_rms(x, gain_lanes, gm):
    ms = _split_dot(x * x, gm)
    return x * lax.rsqrt(ms + EPS) * gain_lanes


def _mem_kv_body(m_ref, g_ref, w_ref, gk_ref, k_ref, v_ref):
    hm = _rms_rows(m_ref[0], g_ref[...]).astype(BF16)
    kv = _dot(hm, w_ref[...])
    k_ref[0] = _head_rms(kv[:, :D_BRANCH], gk_ref[...], _group_mean_matrix()).astype(BF16)
    v_ref[0] = kv[:, D_BRANCH:].astype(BF16)


def _mem_kv(mem, g, wkv, gk_lanes):
    b, m, _ = mem.shape
    out = jax.ShapeDtypeStruct((b, m, D_BRANCH), BF16)
    blk = pl.BlockSpec((1, m, D_BRANCH), lambda bi: (bi, 0, 0))
    return pl.pallas_call(
        _mem_kv_body,
        grid=(b,),
        in_specs=[pl.BlockSpec((1, m, D_MODEL), lambda bi: (bi, 0, 0)), _resident((1, D_MODEL)),
                  _resident((D_MODEL, 2 * D_BRANCH)), _resident((1, D_BRANCH))],
        out_specs=[blk, blk],
        out_shape=[out, out],
        compiler_params=_params("parallel"),
        name="mem_kv",
    )(mem, g, wkv, gk_lanes)


def _xattn_body(x_ref, k_ref, v_ref, g_ref, wq_ref, gq_ref, wo_ref, o_ref):
    m = k_ref.shape[1]
    gm = _group_mean_matrix()
    lane_head = _lane_head()
    for n in range(x_ref.shape[0]):
        x = x_ref[n]
        h = _rms_rows(x, g_ref[...]).astype(BF16)
        q = _head_rms(_dot(h, wq_ref[...]), gq_ref[...], gm).astype(BF16)
        s_all = _dot_nt(q, _head_expand(k_ref[n], lane_head)) * HEAD_DIM ** -0.5
        probs = []
        for hd in range(N_HEADS):
            s = s_all[:, hd * m:(hd + 1) * m]
            e = jnp.exp(s - jnp.max(s, axis=-1, keepdims=True))
            probs.append((e / jnp.sum(e, axis=-1, keepdims=True)).astype(BF16))
        o = _dot(jnp.concatenate(probs, axis=1), _head_expand(v_ref[n], lane_head))
        o_ref[n] = x + _dot(o.astype(BF16), wo_ref[...])


def _xattn(x3, k, v, g, wq, gq_lanes, wo):
    b, s, _ = x3.shape
    m = k.shape[1]
    ts = TOKEN_TILE
    nb = math.gcd(b, XATTN_BATCH)
    xblk = pl.BlockSpec((nb, ts, D_MODEL), lambda bi, i: (bi, i, 0))
    kvblk = pl.BlockSpec((nb, m, D_BRANCH), lambda bi, i: (bi, 0, 0))
    return pl.pallas_call(
        _xattn_body,
        grid=(b // nb, s // ts),
        in_specs=[xblk, kvblk, kvblk, _resident((1, D_MODEL)), _resident((D_MODEL, D_BRANCH)),
                  _resident((1, D_BRANCH)), _resident((D_BRANCH, D_MODEL))],
        out_specs=xblk,
        out_shape=jax.ShapeDtypeStruct((b, s, D_MODEL), F32),
        compiler_params=_params("parallel", "parallel"),
        name="xattn",
    )(x3, k, v, g, wq, gq_lanes, wo)


def _swiglu_rows(h, wgu_ref, wd_ref, acc_ref, between=None):
    for j in range(D_FF // FF_CHUNK):
        cols = slice(j * FF_CHUNK, (j + 1) * FF_CHUNK)
        up_cols = slice(D_FF + j * FF_CHUNK, D_FF + (j + 1) * FF_CHUNK)
        a = (_silu(_dot(h, wgu_ref[:, cols])) * _dot(h, wgu_ref[:, up_cols])).astype(BF16)
        part = _dot(a, wd_ref[cols, :])
        if j == 0:
            acc_ref[...] = part
        else:
            acc_ref[...] += part
        if between is not None:
            between(j)


def _ffn_body(x_ref, g_ref, wgu_ref, wd_ref, o_ref, acc_ref):
    x = x_ref[...]
    h = _rms_rows(x, g_ref[...]).astype(BF16)
    _swiglu_rows(h, wgu_ref, wd_ref, acc_ref)
    o_ref[...] = x + acc_ref[...]


def _ffn(x2, g, wgu, wd):
    t = x2.shape[0]
    tm = TOKEN_TILE
    rows = pl.BlockSpec((tm, D_MODEL), lambda i: (i, 0))
    return pl.pallas_call(
        _ffn_body,
        grid=(t // tm,),
        in_specs=[rows, _resident((1, D_MODEL)), _resident((D_MODEL, 2 * D_FF)),
                  _resident((D_FF, D_MODEL))],
        out_specs=rows,
        out_shape=jax.ShapeDtypeStruct((t, D_MODEL), F32),
        scratch_shapes=[pltpu.VMEM((tm, D_MODEL), F32)],
        compiler_params=_params("parallel"),
        name="ffn",
    )(x2, g, wgu, wd)


LANE_TILES = D_MODEL // 128


def _to_token_tiles(x, o_ref):
    n = x.shape[0]
    for c in range(LANE_TILES):
        o_ref[pl.ds(c, n, stride=LANE_TILES), :] = x[:, c * 128:(c + 1) * 128]


def _from_token_tiles(ref, n):
    return jnp.concatenate([ref[pl.ds(c, n, stride=LANE_TILES), :] for c in range(LANE_TILES)], axis=1)


def _router_body(x_ref, g_ref, rhi_ref, rlo_ref, o_ref, xt_ref):
    x = x_ref[...]
    _to_token_tiles(x, xt_ref)
    h = _rms_rows(x, g_ref[...])
    hi = h.astype(BF16)
    lo = (h - hi.astype(F32)).astype(BF16)
    logits = _dot(hi, rhi_ref[...]) + (_dot(hi, rlo_ref[...]) + _dot(lo, rhi_ref[...]))
    lane = lax.broadcasted_iota(jnp.int32, logits.shape, 1)
    neg = jnp.float32(-jnp.inf)
    logits = jnp.where(lane < N_EXPERTS, logits, neg)
    m1 = jnp.max(logits, axis=-1, keepdims=True)
    i1 = jnp.min(jnp.where(logits == m1, lane, ROUTER_LANES), axis=-1, keepdims=True)
    rest = jnp.where(lane == i1, neg, logits)
    m2 = jnp.max(rest, axis=-1, keepdims=True)
    i2 = jnp.min(jnp.where(rest == m2, lane, ROUTER_LANES), axis=-1, keepdims=True)
    e2 = jnp.exp(m2 - m1)
    w1 = 1.0 / (1.0 + e2)
    w2 = e2 / (1.0 + e2)
    out = jnp.where(lane == 0, i1.astype(F32), 0.0)
    out = jnp.where(lane == 1, i2.astype(F32), out)
    out = jnp.where(lane == 2, w1, out)
    o_ref[...] = jnp.where(lane == 3, w2, out)


def _router(x2, g, r_hi, r_lo):
    t = x2.shape[0]
    tm = TOKEN_TILE
    return pl.pallas_call(
        _router_body,
        grid=(t // tm,),
        in_specs=[pl.BlockSpec((tm, D_MODEL), lambda i: (i, 0)), _resident((1, D_MODEL)),
                  _resident((D_MODEL, ROUTER_LANES)), _resident((D_MODEL, ROUTER_LANES))],
        out_specs=[pl.BlockSpec((tm, ROUTER_LANES), lambda i: (i, 0)),
                   pl.BlockSpec((tm * LANE_TILES, 128), lambda i: (i, 0))],
        out_shape=[jax.ShapeDtypeStruct((t, ROUTER_LANES), F32),
                   jax.ShapeDtypeStruct((t * LANE_TILES, 128), F32)],
        compiler_params=_params("parallel"),
        name="router",
    )(x2, g, r_hi, r_lo)


def _experts_body(tile_expert_ref, n_tiles_ref, src_ref, src_next_ref, dst_prev_ref, dst_ref,
                  x_hbm, g_ref, wgu_ref, wd_ref, y_hbm, xbuf_ref, ybuf_ref, acc_ref, gsem, ssem):
    i = pl.program_id(0)
    last = pl.num_programs(0) - 1
    slot = lax.rem(i, 2)
    other = 1 - slot
    tm = EXPERT_TILE
    rows = LANE_TILES

    def gather(idx_ref, r, s):
        src = pl.multiple_of(idx_ref[0, 0, r] * rows, rows)
        return pltpu.make_async_copy(x_hbm.at[pl.ds(src, rows)],
                                     xbuf_ref.at[s, pl.ds(r * rows, rows)], gsem)

    def scatter(idx_ref, r, s):
        dst = pl.multiple_of(idx_ref[0, 0, r] * rows, rows)
        return pltpu.make_async_copy(ybuf_ref.at[s, pl.ds(r * rows, rows)],
                                     y_hbm.at[pl.ds(dst, rows)], ssem)

    def wait_gather():
        pltpu.make_async_copy(x_hbm.at[pl.ds(0, tm * rows)], xbuf_ref.at[0], gsem).wait()

    def wait_scatter():
        pltpu.make_async_copy(ybuf_ref.at[0], y_hbm.at[pl.ds(0, tm * rows)], ssem).wait()

    def start_next(r, lane=0):
        gather(src_next_ref, r, other).start(priority=lane)
        scatter(dst_prev_ref, r, other).start(priority=1 - lane)

    @pl.when(i == 0)
    def _():
        ybuf_ref[...] = jnp.zeros_like(ybuf_ref)
        lax.fori_loop(0, tm, lambda r, c: (gather(src_ref, r, 0).start(), c)[1], 0)

    wait_gather()

    @pl.when(i > 0)
    def _():
        wait_scatter()

    @pl.when(i < n_tiles_ref[0])
    def _():
        h = _rms_rows(_from_token_tiles(xbuf_ref.at[slot], tm), g_ref[...]).astype(BF16)
        per = -(-tm // EXPERT_ISSUE_CHUNKS)

        def between(j):
            for r in range(j * per, min((j + 1) * per, tm)):
                start_next(r, r % 2)

        _swiglu_rows(h, wgu_ref.at[0], wd_ref.at[0], acc_ref, between)
        _to_token_tiles(acc_ref[...], ybuf_ref.at[slot])

    @pl.when(i >= n_tiles_ref[0])
    def _():
        ybuf_ref[slot] = jnp.zeros((tm * rows, 128), F32)
        lax.fori_loop(0, tm, lambda r, c: (start_next(r), c)[1], 0)

    @pl.when(i == last)
    def _():
        wait_gather()
        wait_scatter()
        lax.fori_loop(0, tm, lambda r, c: (scatter(dst_ref, r, slot).start(), c)[1], 0)
        wait_scatter()


def _experts(x_tiles, src_tiles, dst_tiles, tile_expert, n_tiles, g, wgu, wd):
    n_steps = src_tiles.shape[0] - 1
    tm = EXPERT_TILE

    def idx(off):
        return pl.BlockSpec((1, 1, tm), lambda i, te, nt: (i + off, 0, 0), memory_space=pltpu.SMEM)

    def expert(shape):
        return pl.BlockSpec((1,) + shape, lambda i, te, nt: (te[i], 0, 0),
                            pipeline_mode=pl.Buffered(1))

    grid_spec = pltpu.PrefetchScalarGridSpec(
        num_scalar_prefetch=2,
        grid=(n_steps,),
        in_specs=[idx(0), idx(1), idx(0), idx(1),
                  pl.BlockSpec(memory_space=pl.ANY),
                  pl.BlockSpec((1, D_MODEL), lambda i, te, nt: (0, 0)),
                  expert((D_MODEL, 2 * D_FF)), expert((D_FF, D_MODEL))],
        out_specs=pl.BlockSpec(memory_space=pl.ANY),
        scratch_shapes=[pltpu.VMEM((2, tm * LANE_TILES, 128), F32),
                        pltpu.VMEM((2, tm * LANE_TILES, 128), F32),
                        pltpu.VMEM((tm, D_MODEL), F32),
                        pltpu.SemaphoreType.DMA(()), pltpu.SemaphoreType.DMA(())],
    )
    return pl.pallas_call(
        _experts_body,
        grid_spec=grid_spec,
        out_shape=jax.ShapeDtypeStruct(((n_steps + 1) * tm * LANE_TILES, 128), F32),
        compiler_params=_params("arbitrary"),
        name="experts",
    )(tile_expert, n_tiles, src_tiles, src_tiles, dst_tiles, dst_tiles, x_tiles, g, wgu, wd)


def _combine_body(x_ref, y0_ref, y1_ref, r_ref, o_ref):
    routed = r_ref[...]
    tm = x_ref.shape[0]
    y0 = _from_token_tiles(y0_ref, tm)
    y1 = _from_token_tiles(y1_ref, tm)
    o_ref[...] = x_ref[...] + (routed[:, 2:3] * y0 + routed[:, 3:4] * y1)


def _combine(x2, y_tiles, routed):
    t = x2.shape[0]
    tm = TOKEN_TILE
    rows = pl.BlockSpec((tm, D_MODEL), lambda i: (i, 0))
    return pl.pallas_call(
        _combine_body,
        grid=(t // tm,),
        in_specs=[rows, pl.BlockSpec((tm * LANE_TILES, 128), lambda i: (i, 0)),
                  pl.BlockSpec((tm * LANE_TILES, 128), lambda i: (i + t // tm, 0)),
                  pl.BlockSpec((tm, ROUTER_LANES), lambda i: (i, 0))],
        out_specs=rows,
        out_shape=jax.ShapeDtypeStruct((t, D_MODEL), F32),
        compiler_params=_params("parallel"),
        name="combine",
    )(x2, y_tiles, y_tiles, routed)


def _moe(x2, g, router, wgu, wd):
    t = x2.shape[0]
    tm = EXPERT_TILE
    r_pad = jnp.pad(router, ((0, 0), (0, ROUTER_LANES - N_EXPERTS)))
    r_hi = r_pad.astype(BF16)
    r_lo = (r_pad - r_hi.astype(F32)).astype(BF16)
    routed, x_tiles = _router(x2, g, r_hi, r_lo)
    expert_of = routed[:, 0:2].astype(jnp.int32).T.reshape(-1)

    n_steps = (2 * t) // tm + N_EXPERTS
    n_pad = n_steps * tm - 2 * t
    counts = jnp.sum((expert_of[:, None] == jnp.arange(N_EXPERTS, dtype=jnp.int32)[None, :])
                     .astype(jnp.int32), axis=0)
    padded = ((counts + tm - 1) // tm) * tm
    pad_ends = jnp.cumsum(padded - counts)
    filler = jnp.arange(n_pad, dtype=jnp.int32)
    filler_expert = jnp.sum((filler[:, None] >= pad_ends[None, :]).astype(jnp.int32), axis=1)
    shift = 1 + (2 * t - 1).bit_length()
    keys = jnp.concatenate([expert_of * (1 << shift) + jnp.arange(2 * t, dtype=jnp.int32),
                            filler_expert * (1 << shift) + (1 << (shift - 1)) + filler])
    vals = jnp.concatenate([jnp.arange(2 * t, dtype=jnp.int32), 2 * t + tm + filler])
    _, dst = lax.sort((keys, vals), num_keys=1)
    src = jnp.where(dst < 2 * t, dst % t, 0)
    spare_tile = 2 * t + jnp.arange(tm, dtype=jnp.int32)
    dst_tiles = jnp.concatenate([spare_tile, dst]).reshape(n_steps + 1, 1, tm)
    src_tiles = jnp.concatenate([src, jnp.zeros((tm,), jnp.int32)]).reshape(n_steps + 1, 1, tm)
    ends = jnp.cumsum(padded)
    n_tiles = (ends[-1] // tm).astype(jnp.int32).reshape(1)
    tile_start = jnp.minimum(jnp.arange(n_steps, dtype=jnp.int32), n_tiles[0] - 1) * tm
    tile_expert = jnp.sum((tile_start[:, None] >= ends[None, :]).astype(jnp.int32), axis=1)

    y_tiles = _experts(x_tiles, src_tiles, dst_tiles, tile_expert, n_tiles, g, wgu, wd)
    return _combine(x2, y_tiles, routed)


def _block_diag(w):
    g, n, _ = w.shape
    eye = jnp.eye(g, dtype=w.dtype)
    return (eye[:, None, :, None] * w[:, :, None, :]).reshape(g * n, g * n)


def kernel(x, mem, positions, norm_mix_g, w_in, conv_w, conv_b, conv_ln_g, conv_ln_b, pool_w, pool_scale, w_gate, w_branch, w_out, norm_xa_g, norm_mem_g, xa_wq, xa_wkv, xa_q_norm_g, xa_k_norm_g, xa_wo, norm_ffn_g, ffn_w_gu, ffn_w_down, moe_router, moe_w_gu, moe_w_down):
    b, s, d = x.shape
    depth = w_in.shape[0]
    t = b * s
    assert d == D_MODEL and s % TOKEN_TILE == 0 and s % SEQ_TILE == 0

    def row(v):
        return v.astype(F32)[None, :]

    cos_t, sin_t = _rope_tables(positions)
    x2 = x.astype(F32).reshape(t, d)
    for l in range(depth):
        cw = jnp.pad(conv_w[l].astype(F32), ((0, SEQ_HALO - CONV_WIDTH), (0, 0)))
        ycp, u_att = _mix_front(x2.reshape(b, s, d), row(norm_mix_g[l]), w_in[l].astype(BF16), cw,
                                row(conv_b[l]), row(conv_ln_g[l]), row(conv_ln_b[l]),
                                _block_diag(pool_w[l]).astype(BF16), row(pool_scale[l]))
        ysb = _stickbreak(u_att)
        yret = _retention(u_att, cos_t, sin_t)
        x2 = _merge(x2, ycp.reshape(t, 2 * D_BRANCH), ysb.reshape(t, D_BRANCH),
                    yret.reshape(t, D_BRANCH), row(norm_mix_g[l]), w_gate[l].astype(BF16),
                    w_branch[l].astype(BF16), w_out[l].astype(BF16))
        k_mem, v_mem = _mem_kv(mem.astype(F32), row(norm_mem_g[l]), xa_wkv[l].astype(BF16),
                               row(jnp.tile(xa_k_norm_g[l], N_HEADS)))
        x2 = _xattn(x2.reshape(b, s, d), k_mem, v_mem, row(norm_xa_g[l]), xa_wq[l].astype(BF16),
                    row(jnp.tile(xa_q_norm_g[l], N_HEADS)), xa_wo[l].astype(BF16)).reshape(t, d)
        g_ffn = row(norm_ffn_g[l])
        if l % 2 == 0:
            x2 = _ffn(x2, g_ffn, ffn_w_gu[l // 2].astype(BF16), ffn_w_down[l // 2].astype(BF16))
        else:
            x2 = _moe(x2, g_ffn, moe_router[l // 2].astype(F32), moe_w_gu[l // 2].astype(BF16),
                      moe_w_down[l // 2].astype(BF16))
    return x2.reshape(b, s, d).astype(x.dtype)
```

```python
import math

import jax
import jax.numpy as jnp
from jax import lax
from jax.experimental import pallas as pl
from jax.experimental.pallas import tpu as pltpu

F32 = jnp.float32
BF16 = jnp.bfloat16

D_MODEL = 1024
HEAD_DIM = 64
N_HEADS = 4
D_BRANCH = N_HEADS * HEAD_DIM
CONV_WIDTH = 31
POOL_WINDOWS = (2, 4, 8, 16)
D_FF = 2816
N_EXPERTS = 8
ROPE_THETA = 10000.0
EPS = 1e-6

COL_CONV = 0
COL_POOL = 2
D_IN = 10 * D_BRANCH
COL_SB_Q, COL_SB_K, COL_SB_V = 0, 1, 2
COL_RET_Q, COL_RET_K, COL_RET_V, COL_RET_G = 3, 4, 5, 6
D_ATT = 7 * D_BRANCH

V7X_VMEM_BYTES = 64 * 1024 * 1024
VMEM_LIMIT_BYTES = V7X_VMEM_BYTES - 8 * 1024 * 1024

TOKEN_TILE = 512
SEQ_TILE = 256
SB_QUERY_TILE = 512
SB_KEY_TILE = 256
SB_CLAMP = 30.0
LOG2E = 1.4426950408889634
RET_BATCH = 8
XATTN_BATCH = 4
SEQ_HALO = 32
CONV_ROWS = 64
FF_CHUNK = 256
EXPERT_TILE = 512
EXPERT_ISSUE_CHUNKS = 6
ROUTER_LANES = 128


def _resident(shape):
    return pl.BlockSpec(shape, lambda *_: (0,) * len(shape), pipeline_mode=pl.Buffered(1))


def _params(*semantics):
    return pltpu.CompilerParams(dimension_semantics=semantics, vmem_limit_bytes=VMEM_LIMIT_BYTES)


def _sigmoid(x):
    return 0.5 * jnp.tanh(0.5 * x) + 0.5


def _silu(x):
    return x * _sigmoid(x)


def _rms_rows(x, g):
    ms = jnp.mean(x * x, axis=-1, keepdims=True)
    return x * lax.rsqrt(ms + EPS) * g


def _dot(a, b):
    return jnp.dot(a, b, preferred_element_type=F32)


def _dot_nt(a, b):
    return lax.dot_general(a, b, (((1,), (1,)), ((), ())), preferred_element_type=F32)


def _dot_tn(a, b):
    return lax.dot_general(a, b, (((0,), (0,)), ((), ())), preferred_element_type=F32)


def _split_dot(x, w):
    hi = x.astype(BF16)
    lo = (x - hi.astype(F32)).astype(BF16)
    return _dot(hi, w) + _dot(lo, w)


def _head_expand(x, lane_head):
    zero = jnp.zeros_like(x)
    return jnp.concatenate([jnp.where(lane_head == h, x, zero) for h in range(N_HEADS)], axis=0)


def _lane_head(width=D_BRANCH):
    return lax.shift_right_logical(lax.broadcasted_iota(jnp.int32, (1, width), 1), 6)


def _group_mean_matrix():
    r = lax.broadcasted_iota(jnp.int32, (D_BRANCH, D_BRANCH), 0)
    c = lax.broadcasted_iota(jnp.int32, (D_BRANCH, D_BRANCH), 1)
    same = lax.shift_right_logical(r, 6) == lax.shift_right_logical(c, 6)
    return jnp.where(same, 1.0 / HEAD_DIM, 0.0).astype(BF16)


def _mix_front_body(x_ref, g_ref, w_ref, cw_ref, cb_ref, lg_ref, lb_ref, pw_ref, ps_ref,
                    y_ref, ua_ref, vs_ref, sh_ref, co_ref, pp_ref, q_ref):
    ts = x_ref.shape[1]
    i = pl.program_id(1)
    h = _rms_rows(x_ref[0], g_ref[...]).astype(BF16)
    att = (COL_POOL + 1) * D_BRANCH
    u = _dot(h, w_ref[:, 0:att])
    ua = _dot(h, w_ref[:, att:])
    ua_ref[0, :, 0:D_BRANCH] = (ua[:, 0:D_BRANCH] * (LOG2E * HEAD_DIM ** -0.5)).astype(BF16)
    ua_ref[0, :, D_BRANCH:] = ua[:, D_BRANCH:].astype(BF16)

    @pl.when(i == 0)
    def _():
        vs_ref[0:SEQ_HALO, :] = jnp.zeros((SEQ_HALO, D_BRANCH), F32)
        pp_ref[0:SEQ_HALO, :] = jnp.zeros((SEQ_HALO, D_BRANCH), F32)

    vs_ref[SEQ_HALO:SEQ_HALO + ts, :] = u[:, 0:D_BRANCH] * _sigmoid(u[:, D_BRANCH:2 * D_BRANCH])
    span = ts + SEQ_HALO - 8
    for r in range(1, 8):
        sh_ref[r, 0:span, :] = vs_ref[r:r + span, :]
    lead = SEQ_HALO - (CONV_WIDTH - 1)
    for r0 in range(0, ts, CONV_ROWS):
        acc = jnp.zeros((CONV_ROWS, D_BRANCH), F32)
        for k in range(CONV_WIDTH):
            a, r = divmod(lead + k, 8)
            rows = slice(r0 + 8 * a, r0 + 8 * a + CONV_ROWS)
            tap = vs_ref[rows, :] if r == 0 else sh_ref[r, rows, :]
            acc = acc + cw_ref[k:k + 1, :] * tap
        co_ref[r0:r0 + CONV_ROWS, :] = acc
    c = co_ref[...] + cb_ref[...]
    mu = jnp.mean(c, axis=-1, keepdims=True)
    cc = c - mu
    var = jnp.mean(cc * cc, axis=-1, keepdims=True)
    y_ref[0, :, 0:D_BRANCH] = _silu(cc * lax.rsqrt(var + EPS) * lg_ref[...] + lb_ref[...]).astype(BF16)
    vs_ref[0:SEQ_HALO, :] = vs_ref[ts:ts + SEQ_HALO, :]

    up = u[:, COL_POOL * D_BRANCH:att]
    pp_ref[SEQ_HALO:SEQ_HALO + ts, :] = up
    end = SEQ_HALO + ts
    t_pos = (i * ts + lax.broadcasted_iota(jnp.int32, (ts, 1), 0)).astype(F32) + 1.0
    lane = lax.broadcasted_iota(jnp.int32, (1, 128), 1)
    means = []
    for half in range(2):
        cols = slice(half * 128, (half + 1) * 128)
        q_ref[0, 8:end, cols] = pp_ref[8:end, cols] + pp_ref[7:end - 1, cols]
        q_ref[1, 16:end, cols] = q_ref[0, 16:end, cols] + q_ref[0, 14:end - 2, cols]
        if half == 0:
            small, big = q_ref[0, SEQ_HALO:end, cols], q_ref[1, SEQ_HALO:end, cols]
        else:
            q_ref[2, 24:end, cols] = q_ref[1, 24:end, cols] + q_ref[1, 20:end - 4, cols]
            small = q_ref[2, SEQ_HALO:end, cols]
            big = small + q_ref[2, SEQ_HALO - 8:end - 8, cols]
        w_small, w_big = POOL_WINDOWS[2 * half], POOL_WINDOWS[2 * half + 1]
        m_small = small / jnp.minimum(t_pos, float(w_small))
        m_big = big / jnp.minimum(t_pos, float(w_big))
        means.append(jnp.where(lane < HEAD_DIM, m_small, m_big))
    d = (jnp.concatenate(means, axis=1) - up).astype(BF16)
    y_ref[0, :, D_BRANCH:2 * D_BRANCH] = (_dot(d, pw_ref[...]) * ps_ref[...]).astype(BF16)
    pp_ref[0:SEQ_HALO, :] = pp_ref[ts:ts + SEQ_HALO, :]


def _mix_front(x3, g, w, conv_w, conv_b, ln_g, ln_b, pool_w_bd, pool_scale):
    b, s, _ = x3.shape
    ts = TOKEN_TILE
    return pl.pallas_call(
        _mix_front_body,
        grid=(b, s // ts),
        in_specs=[pl.BlockSpec((1, ts, D_MODEL), lambda bi, i: (bi, i, 0)),
                  _resident((1, D_MODEL)), _resident((D_MODEL, D_IN)),
                  _resident((SEQ_HALO, D_BRANCH)), _resident((1, D_BRANCH)),
                  _resident((1, D_BRANCH)), _resident((1, D_BRANCH)),
                  _resident((D_BRANCH, D_BRANCH)), _resident((1, D_BRANCH))],
        out_specs=[pl.BlockSpec((1, ts, 2 * D_BRANCH), lambda bi, i: (bi, i, 0)),
                   pl.BlockSpec((1, ts, D_ATT), lambda bi, i: (bi, i, 0))],
        out_shape=[jax.ShapeDtypeStruct((b, s, 2 * D_BRANCH), BF16),
                   jax.ShapeDtypeStruct((b, s, D_ATT), BF16)],
        scratch_shapes=[pltpu.VMEM((SEQ_HALO + ts, D_BRANCH), F32),
                        pltpu.VMEM((8, SEQ_HALO + ts, D_BRANCH), F32),
                        pltpu.VMEM((ts, D_BRANCH), F32),
                        pltpu.VMEM((SEQ_HALO + ts, D_BRANCH), F32),
                        pltpu.VMEM((3, SEQ_HALO + ts, D_BRANCH), F32)],
        compiler_params=_params("arbitrary", "arbitrary"),
        name="mix_front",
    )(x3, g, w, conv_w, conv_b, ln_g, ln_b, pool_w_bd, pool_scale)


def _stickbreak_body(q_ref, k_ref, v_ref, o_ref, kexp_ref, vexp_ref, w_ref, p_ref, f_ref, nr_ref,
                     acc_ref):
    tq = q_ref.shape[1]
    tk = SB_KEY_TILE
    n_kb = k_ref.shape[1] // tk
    per_q = tq // tk
    assert per_q % 2 == 0
    qi = pl.program_id(1)
    lane_head = _lane_head()

    @pl.when(qi == 0)
    def _():
        row_head = lax.shift_right_logical(lax.broadcasted_iota(jnp.int32, (D_BRANCH, 1), 0), 6)

        def expand(j, carry):
            start = pl.multiple_of(j * tk, tk)
            kt = k_ref[0, pl.ds(start, tk), :].astype(F32).T
            kexp_ref[j] = jnp.concatenate(
                [jnp.where(row_head == h, kt, 0.0) for h in range(N_HEADS)], axis=1).astype(BF16)
            vexp_ref[j] = _head_expand(v_ref[0, pl.ds(start, tk), :], lane_head)
            return carry

        lax.fori_loop(0, n_kb, expand, 0)

    q = q_ref[0]
    row = lax.broadcasted_iota(jnp.int32, (tk, tk), 0)
    col = lax.broadcasted_iota(jnp.int32, (tk, tk), 1)
    tri = (row >= col).astype(BF16)

    n_blocks = (qi + 1) * per_q

    def key_block(j):
        return jnp.maximum(n_blocks - 1 - j, 0)

    def logits(j, slot, r0=0):
        w_ref[slot, r0:, :] = _dot(q[r0:], kexp_ref[key_block(j)])

    def probs(slot, mask, r0=0):
        neg_run = nr_ref[r0:, :]
        f_ref[slot, r0:, :] = jnp.exp2(neg_run)
        total = None
        for h in range(N_HEADS):
            w = w_ref[slot, r0:, h * tk:(h + 1) * tk]
            sp = jnp.maximum(jnp.log2(1.0 + jnp.exp2(jnp.minimum(w, SB_CLAMP))), w)
            if mask is not None:
                sp = jnp.where(mask, sp, 0.0)
            rev = _dot(sp.astype(BF16), tri)
            p = jnp.exp2(w - rev)
            if mask is not None:
                p = jnp.where(mask, p, 0.0)
            p_ref[slot, r0:, h * tk:(h + 1) * tk] = p.astype(BF16)
            total = rev[:, 0:1] if total is None else jnp.where(lane_head == h, rev[:, 0:1], total)
        nr_ref[r0:, :] = neg_run - total

    def accumulate(j, slot, r0=0):
        acc_ref[r0:, :] += _dot(p_ref[slot, r0:, :], vexp_ref[key_block(j)]) * f_ref[slot, r0:, :]

    acc_ref[...] = jnp.zeros_like(acc_ref)
    nr_ref[...] = jnp.zeros_like(nr_ref)
    t_loc = lax.broadcasted_iota(jnp.int32, (tq, tk), 0)
    s_loc = lax.broadcasted_iota(jnp.int32, (tq, tk), 1)

    def first_row(j):
        return max(per_q - 1 - j, 0) * tk

    logits(0, 0, first_row(0))
    for j in range(per_q):
        logits(j + 1, (j + 1) % 2, first_row(j + 1))
        r0 = first_row(j)
        probs(j % 2, ((r0 + s_loc) < t_loc)[r0:], r0)
        if j > 0:
            accumulate(j - 1, (j - 1) % 2, first_row(j - 1))

    def pair(it, carry):
        j = per_q + 2 * it
        logits(j + 1, 1)
        probs(0, None)
        accumulate(j - 1, 1)
        logits(j + 2, 0)
        probs(1, None)
        accumulate(j, 0)
        return carry

    lax.fori_loop(0, qi * (per_q // 2), pair, 0)
    accumulate(n_blocks - 1, 1)
    o_ref[0] = acc_ref[...].astype(BF16)


def _stickbreak(u3):
    b, s, _ = u3.shape
    tq, tk = SB_QUERY_TILE, SB_KEY_TILE
    return pl.pallas_call(
        _stickbreak_body,
        grid=(b, s // tq),
        in_specs=[pl.BlockSpec((1, tq, D_BRANCH), lambda bi, i: (bi, i, COL_SB_Q)),
                  pl.BlockSpec((1, s, D_BRANCH), lambda bi, i: (bi, 0, COL_SB_K)),
                  pl.BlockSpec((1, s, D_BRANCH), lambda bi, i: (bi, 0, COL_SB_V))],
        out_specs=pl.BlockSpec((1, tq, D_BRANCH), lambda bi, i: (bi, i, 0)),
        out_shape=jax.ShapeDtypeStruct((b, s, D_BRANCH), BF16),
        scratch_shapes=[pltpu.VMEM((s // tk, D_BRANCH, N_HEADS * tk), BF16),
                        pltpu.VMEM((s // tk, N_HEADS * tk, D_BRANCH), BF16),
                        pltpu.VMEM((2, tq, N_HEADS * tk), F32),
                        pltpu.VMEM((2, tq, N_HEADS * tk), BF16),
                        pltpu.VMEM((2, tq, D_BRANCH), F32),
                        pltpu.VMEM((tq, D_BRANCH), F32),
                        pltpu.VMEM((tq, D_BRANCH), F32)],
        compiler_params=_params("arbitrary", "arbitrary"),
        name="stickbreak",
    )(u3, u3, u3)


def _rope_body(pos_ref, f_ref, cos_ref, sin_ref):
    ang = pos_ref[0].astype(F32) * f_ref[...]
    cos_ref[0] = jnp.cos(ang)
    sin_ref[0] = jnp.sin(ang)


def _rope_tables(positions):
    b, s = positions.shape
    half = HEAD_DIM // 2
    inv_freq = ROPE_THETA ** (-jnp.arange(half, dtype=F32) / half)
    freq = jnp.tile(inv_freq, 128 // half)[None, :]
    ts = SEQ_TILE
    out = jax.ShapeDtypeStruct((b, s, 128), F32)
    return pl.pallas_call(
        _rope_body,
        grid=(b, s // ts),
        in_specs=[pl.BlockSpec((1, ts, 1), lambda bi, i: (bi, i, 0)), _resident((1, 128))],
        out_specs=[pl.BlockSpec((1, ts, 128), lambda bi, i: (bi, i, 0))] * 2,
        out_shape=[out, out],
        compiler_params=_params("parallel", "parallel"),
        name="rope_tables",
    )(positions[:, :, None], freq)


def _retention_body(q_ref, k_ref, v_ref, g_ref, cos_ref, sin_ref, dec_ref, xi_ref, zeta_ref,
                    gam_ref, o_ref, state_ref):
    @pl.when(pl.program_id(1) == 0)
    def _():
        state_ref[...] = jnp.zeros_like(state_ref)

    lane = lax.broadcasted_iota(jnp.int32, (1, D_BRANCH), 1)
    lane_head = lax.shift_right_logical(lane, 6)
    first_half = (lane & (HEAD_DIM - 1)) < HEAD_DIM // 2
    r = lax.broadcasted_iota(jnp.int32, (D_BRANCH, D_BRANCH), 0)
    c = lax.broadcasted_iota(jnp.int32, (D_BRANCH, D_BRANCH), 1)
    same_head = lax.shift_right_logical(r, 6) == lax.shift_right_logical(c, 6)
    gm = _group_mean_matrix()

    for n in range(q_ref.shape[0]):
        cos = jnp.concatenate([cos_ref[n], cos_ref[n]], axis=1)
        sin = jnp.concatenate([sin_ref[n], sin_ref[n]], axis=1)

        def rope(x):
            partner = jnp.where(first_half, -pltpu.roll(x, D_BRANCH - HEAD_DIM // 2, 1),
                                pltpu.roll(x, HEAD_DIM // 2, 1))
            return x * cos + partner * sin

        q = rope(q_ref[n].astype(F32))
        k = rope(k_ref[n].astype(F32)) * HEAD_DIM ** -0.5
        v = v_ref[n]
        kexp = _head_expand(k.astype(BF16), lane_head)
        vexp = _head_expand(v, lane_head)
        scores = _dot_nt(q.astype(BF16), kexp) * dec_ref[...]
        inner = _dot(scores.astype(BF16), vexp)
        state = state_ref[n]
        cross = _dot((q * xi_ref[...]).astype(BF16), state.astype(BF16))
        update = _dot_tn((k * zeta_ref[...]).astype(BF16), v)
        state_ref[n] = state * gam_ref[...] + jnp.where(same_head, update, 0.0)

        o = inner + cross
        oc = o - _split_dot(o, gm)
        var = _split_dot(oc * oc, gm)
        o_ref[n] = (_silu(g_ref[n].astype(F32)) * (oc * lax.rsqrt(var + EPS))).astype(BF16)


def _retention(u3, cos_t, sin_t):
    b, s, _ = u3.shape
    tc = SEQ_TILE
    gammas = 1.0 - jnp.exp2(-5.0 - jnp.arange(N_HEADS, dtype=F32))
    log_g = jnp.log(gammas)
    log_g_lane = jnp.repeat(log_g, HEAD_DIM)[None, :]
    idx = jnp.arange(tc, dtype=F32)
    rel = idx[:, None] - idx[None, :]
    decay = jnp.where(rel >= 0, jnp.exp(jnp.maximum(rel, 0.0) * log_g[:, None, None]), 0.0)
    decay = jnp.transpose(decay, (1, 0, 2)).reshape(tc, N_HEADS * tc)
    xi = jnp.exp((idx + 1.0)[:, None] * log_g_lane)
    zeta = jnp.exp((tc - 1.0 - idx)[:, None] * log_g_lane)
    gam = jnp.broadcast_to(jnp.exp(tc * log_g_lane).T, (D_BRANCH, D_BRANCH))

    nb = math.gcd(b, RET_BATCH)

    def col(cb):
        return pl.BlockSpec((nb, tc, D_BRANCH), lambda bi, i: (bi, i, cb))

    tab = pl.BlockSpec((nb, tc, 128), lambda bi, i: (bi, i, 0))
    return pl.pallas_call(
        _retention_body,
        grid=(b // nb, s // tc),
        in_specs=[col(COL_RET_Q), col(COL_RET_K), col(COL_RET_V), col(COL_RET_G), tab, tab,
                  _resident((tc, N_HEADS * tc)), _resident((tc, D_BRANCH)),
                  _resident((tc, D_BRANCH)), _resident((D_BRANCH, D_BRANCH))],
        out_specs=pl.BlockSpec((nb, tc, D_BRANCH), lambda bi, i: (bi, i, 0)),
        out_shape=jax.ShapeDtypeStruct((b, s, D_BRANCH), BF16),
        scratch_shapes=[pltpu.VMEM((nb, D_BRANCH, D_BRANCH), F32)],
        compiler_params=_params("parallel", "arbitrary"),
        name="retention",
    )(u3, u3, u3, u3, cos_t, sin_t, decay, xi, zeta, gam)


def _merge_body(x_ref, ycp_ref, ysb_ref, yret_ref, g_ref, wg_ref, wb_ref, wo_ref, o_ref):
    x = x_ref[...]
    h = _rms_rows(x, g_ref[...]).astype(BF16)
    ycp = ycp_ref[...]
    branches = (ycp[:, :D_BRANCH], ycp[:, D_BRANCH:], ysb_ref[...], yret_ref[...])
    parts = []
    for n in range(D_MODEL // D_BRANCH):
        cols = slice(n * D_BRANCH, (n + 1) * D_BRANCH)
        m = None
        for i, y in enumerate(branches):
            term = _sigmoid(_dot(h, wg_ref[i, :, cols])) * _dot(y, wb_ref[i, :, cols])
            m = term if m is None else m + term
        parts.append(m.astype(BF16))
    o_ref[...] = x + _dot(jnp.concatenate(parts, axis=1), wo_ref[...])


def _merge(x2, ycp, ysb, yret, g, wg, wb, wo):
    t = x2.shape[0]
    tm = TOKEN_TILE

    def rows(width):
        return pl.BlockSpec((tm, width), lambda i: (i, 0))

    return pl.pallas_call(
        _merge_body,
        grid=(t // tm,),
        in_specs=[rows(D_MODEL), rows(2 * D_BRANCH), rows(D_BRANCH), rows(D_BRANCH),
                  _resident((1, D_MODEL)), _resident((N_HEADS, D_MODEL, D_MODEL)),
                  _resident((N_HEADS, D_BRANCH, D_MODEL)), _resident((D_MODEL, D_MODEL))],
        out_specs=rows(D_MODEL),
        out_shape=jax.ShapeDtypeStruct((t, D_MODEL), F32),
        compiler_params=_params("parallel"),
        name="merge",
    )(x2, ycp, ysb, yret, g, wg, wb, wo)


def _head_rms(x, gain_lanes, gm):
    ms = _split_dot(x * x, gm)
    return x * lax.rsqrt(ms + EPS) * gain_lanes


def _mem_kv_body(m_ref, g_ref, w_ref, gk_ref, k_ref, v_ref):
    hm = _rms_rows(m_ref[0], g_ref[...]).astype(BF16)
    kv = _dot(hm, w_ref[...])
    k_ref[0] = _head_rms(kv[:, :D_BRANCH], gk_ref[...], _group_mean_matrix()).astype(BF16)
    v_ref[0] = kv[:, D_BRANCH:].astype(BF16)


def _mem_kv(mem, g, wkv, gk_lanes):
    b, m, _ = mem.shape
    out = jax.ShapeDtypeStruct((b, m, D_BRANCH), BF16)
    blk = pl.BlockSpec((1, m, D_BRANCH), lambda bi: (bi, 0, 0))
    return pl.pallas_call(
        _mem_kv_body,
        grid=(b,),
        in_specs=[pl.BlockSpec((1, m, D_MODEL), lambda bi: (bi, 0, 0)), _resident((1, D_MODEL)),
                  _resident((D_MODEL, 2 * D_BRANCH)), _resident((1, D_BRANCH))],
        out_specs=[blk, blk],
        out_shape=[out, out],
        compiler_params=_params("parallel"),
        name="mem_kv",
    )(mem, g, wkv, gk_lanes)


def _xattn_body(x_ref, k_ref, v_ref, g_ref, wq_ref, gq_ref, wo_ref, o_ref):
    m = k_ref.shape[1]
    gm = _group_mean_matrix()
    lane_head = _lane_head()
    for n in range(x_ref.shape[0]):
        x = x_ref[n]
        h = _rms_rows(x, g_ref[...]).astype(BF16)
        q = _head_rms(_dot(h, wq_ref[...]), gq_ref[...], gm).astype(BF16)
        s_all = _dot_nt(q, _head_expand(k_ref[n], lane_head)) * HEAD_DIM ** -0.5
        probs = []
        for hd in range(N_HEADS):
            s = s_all[:, hd * m:(hd + 1) * m]
            e = jnp.exp(s - jnp.max(s, axis=-1, keepdims=True))
            probs.append((e / jnp.sum(e, axis=-1, keepdims=True)).astype(BF16))
        o = _dot(jnp.concatenate(probs, axis=1), _head_expand(v_ref[n], lane_head))
        o_ref[n] = x + _dot(o.astype(BF16), wo_ref[...])


def _xattn(x3, k, v, g, wq, gq_lanes, wo):
    b, s, _ = x3.shape
    m = k.shape[1]
    ts = TOKEN_TILE
    nb = math.gcd(b, XATTN_BATCH)
    xblk = pl.BlockSpec((nb, ts, D_MODEL), lambda bi, i: (bi, i, 0))
    kvblk = pl.BlockSpec((nb, m, D_BRANCH), lambda bi, i: (bi, 0, 0))
    return pl.pallas_call(
        _xattn_body,
        grid=(b // nb, s // ts),
        in_specs=[xblk, kvblk, kvblk, _resident((1, D_MODEL)), _resident((D_MODEL, D_BRANCH)),
                  _resident((1, D_BRANCH)), _resident((D_BRANCH, D_MODEL))],
        out_specs=xblk,
        out_shape=jax.ShapeDtypeStruct((b, s, D_MODEL), F32),
        compiler_params=_params("parallel", "parallel"),
        name="xattn",
    )(x3, k, v, g, wq, gq_lanes, wo)


def _swiglu_rows(h, wgu_ref, wd_ref, acc_ref, between=None):
    for j in range(D_FF // FF_CHUNK):
        cols = slice(j * FF_CHUNK, (j + 1) * FF_CHUNK)
        up_cols = slice(D_FF + j * FF_CHUNK, D_FF + (j + 1) * FF_CHUNK)
        a = (_silu(_dot(h, wgu_ref[:, cols])) * _dot(h, wgu_ref[:, up_cols])).astype(BF16)
        part = _dot(a, wd_ref[cols, :])
        if j == 0:
            acc_ref[...] = part
        else:
            acc_ref[...] += part
        if between is not None:
            between(j)


def _ffn_body(x_ref, g_ref, wgu_ref, wd_ref, o_ref, acc_ref):
    x = x_ref[...]
    h = _rms_rows(x, g_ref[...]).astype(BF16)
    _swiglu_rows(h, wgu_ref, wd_ref, acc_ref)
    o_ref[...] = x + acc_ref[...]


def _ffn(x2, g, wgu, wd):
    t = x2.shape[0]
    tm = TOKEN_TILE
    rows = pl.BlockSpec((tm, D_MODEL), lambda i: (i, 0))
    return pl.pallas_call(
        _ffn_body,
        grid=(t // tm,),
        in_specs=[rows, _resident((1, D_MODEL)), _resident((D_MODEL, 2 * D_FF)),
                  _resident((D_FF, D_MODEL))],
        out_specs=rows,
        out_shape=jax.ShapeDtypeStruct((t, D_MODEL), F32),
        scratch_shapes=[pltpu.VMEM((tm, D_MODEL), F32)],
        compiler_params=_params("parallel"),
        name="ffn",
    )(x2, g, wgu, wd)


LANE_TILES = D_MODEL // 128


def _to_token_tiles(x, o_ref):
    n = x.shape[0]
    for c in range(LANE_TILES):
        o_ref[pl.ds(c, n, stride=LANE_TILES), :] = x[:, c * 128:(c + 1) * 128]


def _from_token_tiles(ref, n):
    return jnp.concatenate([ref[pl.ds(c, n, stride=LANE_TILES), :] for c in range(LANE_TILES)], axis=1)


def _router_body(x_ref, g_ref, rhi_ref, rlo_ref, o_ref, xt_ref):
    x = x_ref[...]
    _to_token_tiles(x, xt_ref)
    h = _rms_rows(x, g_ref[...])
    hi = h.astype(BF16)
    lo = (h - hi.astype(F32)).astype(BF16)
    logits = _dot(hi, rhi_ref[...]) + (_dot(hi, rlo_ref[...]) + _dot(lo, rhi_ref[...]))
    lane = lax.broadcasted_iota(jnp.int32, logits.shape, 1)
    neg = jnp.float32(-jnp.inf)
    logits = jnp.where(lane < N_EXPERTS, logits, neg)
    m1 = jnp.max(logits, axis=-1, keepdims=True)
    i1 = jnp.min(jnp.where(logits == m1, lane, ROUTER_LANES), axis=-1, keepdims=True)
    rest = jnp.where(lane == i1, neg, logits)
    m2 = jnp.max(rest, axis=-1, keepdims=True)
    i2 = jnp.min(jnp.where(rest == m2, lane, ROUTER_LANES), axis=-1, keepdims=True)
    e2 = jnp.exp(m2 - m1)
    w1 = 1.0 / (1.0 + e2)
    w2 = e2 / (1.0 + e2)
    out = jnp.where(lane == 0, i1.astype(F32), 0.0)
    out = jnp.where(lane == 1, i2.astype(F32), out)
    out = jnp.where(lane == 2, w1, out)
    o_ref[...] = jnp.where(lane == 3, w2, out)


def _router(x2, g, r_hi, r_lo):
    t = x2.shape[0]
    tm = TOKEN_TILE
    return pl.pallas_call(
        _router_body,
        grid=(t // tm,),
        in_specs=[pl.BlockSpec((tm, D_MODEL), lambda i: (i, 0)), _resident((1, D_MODEL)),
                  _resident((D_MODEL, ROUTER_LANES)), _resident((D_MODEL, ROUTER_LANES))],
        out_specs=[pl.BlockSpec((tm, ROUTER_LANES), lambda i: (i, 0)),
                   pl.BlockSpec((tm * LANE_TILES, 128), lambda i: (i, 0))],
        out_shape=[jax.ShapeDtypeStruct((t, ROUTER_LANES), F32),
                   jax.ShapeDtypeStruct((t * LANE_TILES, 128), F32)],
        compiler_params=_params("parallel"),
        name="router",
    )(x2, g, r_hi, r_lo)


def _experts_body(tile_expert_ref, n_tiles_ref, src_ref, src_next_ref, dst_prev_ref, dst_ref,
                  x_hbm, g_ref, wgu_ref, wd_ref, y_hbm, xbuf_ref, ybuf_ref, acc_ref, gsem, ssem):
    i = pl.program_id(0)
    last = pl.num_programs(0) - 1
    slot = lax.rem(i, 2)
    other = 1 - slot
    tm = EXPERT_TILE
    rows = LANE_TILES

    def gather(idx_ref, r, s):
        src = pl.multiple_of(idx_ref[0, 0, r] * rows, rows)
        return pltpu.make_async_copy(x_hbm.at[pl.ds(src, rows)],
                                     xbuf_ref.at[s, pl.ds(r * rows, rows)], gsem)

    def scatter(idx_ref, r, s):
        dst = pl.multiple_of(idx_ref[0, 0, r] * rows, rows)
        return pltpu.make_async_copy(ybuf_ref.at[s, pl.ds(r * rows, rows)],
                                     y_hbm.at[pl.ds(dst, rows)], ssem)

    def wait_gather():
        pltpu.make_async_copy(x_hbm.at[pl.ds(0, tm * rows)], xbuf_ref.at[0], gsem).wait()

    def wait_scatter():
        pltpu.make_async_copy(ybuf_ref.at[0], y_hbm.at[pl.ds(0, tm * rows)], ssem).wait()

    def start_next(r, lane=0):
        gather(src_next_ref, r, other).start(priority=lane)
        scatter(dst_prev_ref, r, other).start(priority=1 - lane)

    @pl.when(i == 0)
    def _():
        ybuf_ref[...] = jnp.zeros_like(ybuf_ref)
        lax.fori_loop(0, tm, lambda r, c: (gather(src_ref, r, 0).start(), c)[1], 0)

    wait_gather()

    @pl.when(i > 0)
    def _():
        wait_scatter()

    @pl.when(i < n_tiles_ref[0])
    def _():
        h = _rms_rows(_from_token_tiles(xbuf_ref.at[slot], tm), g_ref[...]).astype(BF16)
        per = -(-tm // EXPERT_ISSUE_CHUNKS)

        def between(j):
            for r in range(j * per, min((j + 1) * per, tm)):
                start_next(r, r % 2)

        _swiglu_rows(h, wgu_ref.at[0], wd_ref.at[0], acc_ref, between)
        _to_token_tiles(acc_ref[...], ybuf_ref.at[slot])

    @pl.when(i >= n_tiles_ref[0])
    def _():
        ybuf_ref[slot] = jnp.zeros((tm * rows, 128), F32)
        lax.fori_loop(0, tm, lambda r, c: (start_next(r), c)[1], 0)

    @pl.when(i == last)
    def _():
        wait_gather()
        wait_scatter()
        lax.fori_loop(0, tm, lambda r, c: (scatter(dst_ref, r, slot).start(), c)[1], 0)
        wait_scatter()


def _experts(x_tiles, src_tiles, dst_tiles, tile_expert, n_tiles, g, wgu, wd):
    n_steps = src_tiles.shape[0] - 1
    tm = EXPERT_TILE

    def idx(off):
        return pl.BlockSpec((1, 1, tm), lambda i, te, nt: (i + off, 0, 0), memory_space=pltpu.SMEM)

    def expert(shape):
        return pl.BlockSpec((1,) + shape, lambda i, te, nt: (te[i], 0, 0),
                            pipeline_mode=pl.Buffered(1))

    grid_spec = pltpu.PrefetchScalarGridSpec(
        num_scalar_prefetch=2,
        grid=(n_steps,),
        in_specs=[idx(0), idx(1), idx(0), idx(1),
                  pl.BlockSpec(memory_space=pl.ANY),
                  pl.BlockSpec((1, D_MODEL), lambda i, te, nt: (0, 0)),
                  expert((D_MODEL, 2 * D_FF)), expert((D_FF, D_MODEL))],
        out_specs=pl.BlockSpec(memory_space=pl.ANY),
        scratch_shapes=[pltpu.VMEM((2, tm * LANE_TILES, 128), F32),
                        pltpu.VMEM((2, tm * LANE_TILES, 128), F32),
                        pltpu.VMEM((tm, D_MODEL), F32),
                        pltpu.SemaphoreType.DMA(()), pltpu.SemaphoreType.DMA(())],
    )
    return pl.pallas_call(
        _experts_body,
        grid_spec=grid_spec,
        out_shape=jax.ShapeDtypeStruct(((n_steps + 1) * tm * LANE_TILES, 128), F32),
        compiler_params=_params("arbitrary"),
        name="experts",
    )(tile_expert, n_tiles, src_tiles, src_tiles, dst_tiles, dst_tiles, x_tiles, g, wgu, wd)


def _combine_body(x_ref, y0_ref, y1_ref, r_ref, o_ref):
    routed = r_ref[...]
    tm = x_ref.shape[0]
    y0 = _from_token_tiles(y0_ref, tm)
    y1 = _from_token_tiles(y1_ref, tm)
    o_ref[...] = x_ref[...] + (routed[:, 2:3] * y0 + routed[:, 3:4] * y1)


def _combine(x2, y_tiles, routed):
    t = x2.shape[0]
    tm = TOKEN_TILE
    rows = pl.BlockSpec((tm, D_MODEL), lambda i: (i, 0))
    return pl.pallas_call(
        _combine_body,
        grid=(t // tm,),
        in_specs=[rows, pl.BlockSpec((tm * LANE_TILES, 128), lambda i: (i, 0)),
                  pl.BlockSpec((tm * LANE_TILES, 128), lambda i: (i + t // tm, 0)),
                  pl.BlockSpec((tm, ROUTER_LANES), lambda i: (i, 0))],
        out_specs=rows,
        out_shape=jax.ShapeDtypeStruct((t, D_MODEL), F32),
        compiler_params=_params("parallel"),
        name="combine",
    )(x2, y_tiles, y_tiles, routed)


def _moe(x2, g, router, wgu, wd):
    t = x2.shape[0]
    tm = EXPERT_TILE
    r_pad = jnp.pad(router, ((0, 0), (0, ROUTER_LANES - N_EXPERTS)))
    r_hi = r_pad.astype(BF16)
    r_lo = (r_pad - r_hi.astype(F32)).astype(BF16)
    routed, x_tiles = _router(x2, g, r_hi, r_lo)
    expert_of = routed[:, 0:2].astype(jnp.int32).T.reshape(-1)

    n_steps = (2 * t) // tm + N_EXPERTS
    n_pad = n_steps * tm - 2 * t
    counts = jnp.sum((expert_of[:, None] == jnp.arange(N_EXPERTS, dtype=jnp.int32)[None, :])
                     .astype(jnp.int32), axis=0)
    padded = ((counts + tm - 1) // tm) * tm
    pad_ends = jnp.cumsum(padded - counts)
    filler = jnp.arange(n_pad, dtype=jnp.int32)
    filler_expert = jnp.sum((filler[:, None] >= pad_ends[None, :]).astype(jnp.int32), axis=1)
    shift = 1 + (2 * t - 1).bit_length()
    keys = jnp.concatenate([expert_of * (1 << shift) + jnp.arange(2 * t, dtype=jnp.int32),
                            filler_expert * (1 << shift) + (1 << (shift - 1)) + filler])
    vals = jnp.concatenate([jnp.arange(2 * t, dtype=jnp.int32), 2 * t + tm + filler])
    _, dst = lax.sort((keys, vals), num_keys=1)
    src = jnp.where(dst < 2 * t, dst % t, 0)
    spare_tile = 2 * t + jnp.arange(tm, dtype=jnp.int32)
    dst_tiles = jnp.concatenate([spare_tile, dst]).reshape(n_steps + 1, 1, tm)
    src_tiles = jnp.concatenate([src, jnp.zeros((tm,), jnp.int32)]).reshape(n_steps + 1, 1, tm)
    ends = jnp.cumsum(padded)
    n_tiles = (ends[-1] // tm).astype(jnp.int32).reshape(1)
    tile_start = jnp.minimum(jnp.arange(n_steps, dtype=jnp.int32), n_tiles[0] - 1) * tm
    tile_expert = jnp.sum((tile_start[:, None] >= ends[None, :]).astype(jnp.int32), axis=1)

    y_tiles = _experts(x_tiles, src_tiles, dst_tiles, tile_expert, n_tiles, g, wgu, wd)
    return _combine(x2, y_tiles, routed)


def _block_diag(w):
    g, n, _ = w.shape
    eye = jnp.eye(g, dtype=w.dtype)
    return (eye[:, None, :, None] * w[:, :, None, :]).reshape(g * n, g * n)


def kernel(x, mem, positions, norm_mix_g, w_in, conv_w, conv_b, conv_ln_g, conv_ln_b, pool_w, pool_scale, w_gate, w_branch, w_out, norm_xa_g, norm_mem_g, xa_wq, xa_wkv, xa_q_norm_g, xa_k_norm_g, xa_wo, norm_ffn_g, ffn_w_gu, ffn_w_down, moe_router, moe_w_gu, moe_w_down):
    b, s, d = x.shape
    depth = w_in.shape[0]
    t = b * s
    assert d == D_MODEL and s % TOKEN_TILE == 0 and s % SEQ_TILE == 0

    def row(v):
        return v.astype(F32)[None, :]

    cos_t, sin_t = _rope_tables(positions)
    x2 = x.astype(F32).reshape(t, d)
    for l in range(depth):
        cw = jnp.pad(conv_w[l].astype(F32), ((0, SEQ_HALO - CONV_WIDTH), (0, 0)))
        ycp, u_att = _mix_front(x2.reshape(b, s, d), row(norm_mix_g[l]), w_in[l].astype(BF16), cw,
                                row(conv_b[l]), row(conv_ln_g[l]), row(conv_ln_b[l]),
                                _block_diag(pool_w[l]).astype(BF16), row(pool_scale[l]))
        ysb = _stickbreak(u_att)
        yret = _retention(u_att, cos_t, sin_t)
        x2 = _merge(x2, ycp.reshape(t, 2 * D_BRANCH), ysb.reshape(t, D_BRANCH),
                    yret.reshape(t, D_BRANCH), row(norm_mix_g[l]), w_gate[l].astype(BF16),
                    w_branch[l].astype(BF16), w_out[l].astype(BF16))
        k_mem, v_mem = _mem_kv(mem.astype(F32), row(norm_mem_g[l]), xa_wkv[l].astype(BF16),
                               row(jnp.tile(xa_k_norm_g[l], N_HEADS)))
        x2 = _xattn(x2.reshape(b, s, d), k_mem, v_mem, row(norm_xa_g[l]), xa_wq[l].astype(BF16),
                    row(jnp.tile(xa_q_norm_g[l], N_HEADS)), xa_wo[l].astype(BF16)).reshape(t, d)
        g_ffn = row(norm_ffn_g[l])
        if l % 2 == 0:
            x2 = _ffn(x2, g_ffn, ffn_w_gu[l // 2].astype(BF16), ffn_w_down[l // 2].astype(BF16))
        else:
            x2 = _moe(x2, g_ffn, moe_router[l // 2].astype(F32), moe_w_gu[l // 2].astype(BF16),
                      moe_w_down[l // 2].astype(BF16))
    return x2.reshape(b, s, d).astype(x.dtype)
```

```python
import math

import jax
import jax.numpy as jnp
from jax import lax
from jax.experimental import pallas as pl
from jax.experimental.pallas import tpu as pltpu

F32 = jnp.float32
BF16 = jnp.bfloat16

D_MODEL = 1024
HEAD_DIM = 64
N_HEADS = 4
D_BRANCH = N_HEADS * HEAD_DIM
CONV_WIDTH = 31
POOL_WINDOWS = (2, 4, 8, 16)
D_FF = 2816
N_EXPERTS = 8
ROPE_THETA = 10000.0
EPS = 1e-6

COL_CONV = 0
COL_POOL = 2
D_IN = 10 * D_BRANCH
COL_SB_Q, COL_SB_K, COL_SB_V = 0, 1, 2
COL_RET_Q, COL_RET_K, COL_RET_V, COL_RET_G = 3, 4, 5, 6
D_ATT = 7 * D_BRANCH

V7X_VMEM_BYTES = 64 * 1024 * 1024
VMEM_LIMIT_BYTES = V7X_VMEM_BYTES - 8 * 1024 * 1024

TOKEN_TILE = 512
SEQ_TILE = 256
SB_QUERY_TILE = 512
SB_KEY_TILE = 256
SB_CLAMP = 30.0
LOG2E = 1.4426950408889634
RET_BATCH = 8
XATTN_BATCH = 4
SEQ_HALO = 32
CONV_ROWS = 64
FF_CHUNK = 256
EXPERT_TILE = 512
EXPERT_ISSUE_CHUNKS = 6
ROUTER_LANES = 128


def _resident(shape):
    return pl.BlockSpec(shape, lambda *_: (0,) * len(shape), pipeline_mode=pl.Buffered(1))


def _params(*semantics):
    return pltpu.CompilerParams(dimension_semantics=semantics, vmem_limit_bytes=VMEM_LIMIT_BYTES)


def _sigmoid(x):
    return 0.5 * jnp.tanh(0.5 * x) + 0.5


def _silu(x):
    return x * _sigmoid(x)


def _rms_rows(x, g):
    ms = jnp.mean(x * x, axis=-1, keepdims=True)
    return x * lax.rsqrt(ms + EPS) * g


def _dot(a, b):
    return jnp.dot(a, b, preferred_element_type=F32)


def _dot_nt(a, b):
    return lax.dot_general(a, b, (((1,), (1,)), ((), ())), preferred_element_type=F32)


def _dot_tn(a, b):
    return lax.dot_general(a, b, (((0,), (0,)), ((), ())), preferred_element_type=F32)


def _split_dot(x, w):
    hi = x.astype(BF16)
    lo = (x - hi.astype(F32)).astype(BF16)
    return _dot(hi, w) + _dot(lo, w)


def _head_expand(x, lane_head):
    zero = jnp.zeros_like(x)
    return jnp.concatenate([jnp.where(lane_head == h, x, zero) for h in range(N_HEADS)], axis=0)


def _lane_head(width=D_BRANCH):
    return lax.shift_right_logical(lax.broadcasted_iota(jnp.int32, (1, width), 1), 6)


def _group_mean_matrix():
    r = lax.broadcasted_iota(jnp.int32, (D_BRANCH, D_BRANCH), 0)
    c = lax.broadcasted_iota(jnp.int32, (D_BRANCH, D_BRANCH), 1)
    same = lax.shift_right_logical(r, 6) == lax.shift_right_logical(c, 6)
    return jnp.where(same, 1.0 / HEAD_DIM, 0.0).astype(BF16)


def _mix_front_body(x_ref, g_ref, w_ref, cw_ref, cb_ref, lg_ref, lb_ref, pw_ref, ps_ref,
                    y_ref, ua_ref, vs_ref, sh_ref, co_ref, pp_ref, q_ref):
    ts = x_ref.shape[1]
    i = pl.program_id(1)
    h = _rms_rows(x_ref[0], g_ref[...]).astype(BF16)
    att = (COL_POOL + 1) * D_BRANCH
    u = _dot(h, w_ref[:, 0:att])
    ua = _dot(h, w_ref[:, att:])
    ua_ref[0, :, 0:D_BRANCH] = (ua[:, 0:D_BRANCH] * (LOG2E * HEAD_DIM ** -0.5)).astype(BF16)
    ua_ref[0, :, D_BRANCH:] = ua[:, D_BRANCH:].astype(BF16)

    @pl.when(i == 0)
    def _():
        vs_ref[0:SEQ_HALO, :] = jnp.zeros((SEQ_HALO, D_BRANCH), F32)
        pp_ref[0:SEQ_HALO, :] = jnp.zeros((SEQ_HALO, D_BRANCH), F32)

    vs_ref[SEQ_HALO:SEQ_HALO + ts, :] = u[:, 0:D_BRANCH] * _sigmoid(u[:, D_BRANCH:2 * D_BRANCH])
    span = ts + SEQ_HALO - 8
    for r in range(1, 8):
        sh_ref[r, 0:span, :] = vs_ref[r:r + span, :]
    lead = SEQ_HALO - (CONV_WIDTH - 1)
    for r0 in range(0, ts, CONV_ROWS):
        acc = jnp.zeros((CONV_ROWS, D_BRANCH), F32)
        for k in range(CONV_WIDTH):
            a, r = divmod(lead + k, 8)
            rows = slice(r0 + 8 * a, r0 + 8 * a + CONV_ROWS)
            tap = vs_ref[rows, :] if r == 0 else sh_ref[r, rows, :]
            acc = acc + cw_ref[k:k + 1, :] * tap
        co_ref[r0:r0 + CONV_ROWS, :] = acc
    c = co_ref[...] + cb_ref[...]
    mu = jnp.mean(c, axis=-1, keepdims=True)
    cc = c - mu
    var = jnp.mean(cc * cc, axis=-1, keepdims=True)
    y_ref[0, :, 0:D_BRANCH] = _silu(cc * lax.rsqrt(var + EPS) * lg_ref[...] + lb_ref[...]).astype(BF16)
    vs_ref[0:SEQ_HALO, :] = vs_ref[ts:ts + SEQ_HALO, :]

    up = u[:, COL_POOL * D_BRANCH:att]
    pp_ref[SEQ_HALO:SEQ_HALO + ts, :] = up
    end = SEQ_HALO + ts
    t_pos = (i * ts + lax.broadcasted_iota(jnp.int32, (ts, 1), 0)).astype(F32) + 1.0
    lane = lax.broadcasted_iota(jnp.int32, (1, 128), 1)
    means = []
    for half in range(2):
        cols = slice(half * 128, (half + 1) * 128)
        q_ref[0, 8:end, cols] = pp_ref[8:end, cols] + pp_ref[7:end - 1, cols]
        q_ref[1, 16:end, cols] = q_ref[0, 16:end, cols] + q_ref[0, 14:end - 2, cols]
        if half == 0:
            small, big = q_ref[0, SEQ_HALO:end, cols], q_ref[1, SEQ_HALO:end, cols]
        else:
            q_ref[2, 24:end, cols] = q_ref[1, 24:end, cols] + q_ref[1, 20:end - 4, cols]
            small = q_ref[2, SEQ_HALO:end, cols]
            big = small + q_ref[2, SEQ_HALO - 8:end - 8, cols]
        w_small, w_big = POOL_WINDOWS[2 * half], POOL_WINDOWS[2 * half + 1]
        m_small = small / jnp.minimum(t_pos, float(w_small))
        m_big = big / jnp.minimum(t_pos, float(w_big))
        means.append(jnp.where(lane < HEAD_DIM, m_small, m_big))
    d = (jnp.concatenate(means, axis=1) - up).astype(BF16)
    y_ref[0, :, D_BRANCH:2 * D_BRANCH] = (_dot(d, pw_ref[...]) * ps_ref[...]).astype(BF16)
    pp_ref[0:SEQ_HALO, :] = pp_ref[ts:ts + SEQ_HALO, :]


def _mix_front(x3, g, w, conv_w, conv_b, ln_g, ln_b, pool_w_bd, pool_scale):
    b, s, _ = x3.shape
    ts = TOKEN_TILE
    return pl.pallas_call(
        _mix_front_body,
        grid=(b, s // ts),
        in_specs=[pl.BlockSpec((1, ts, D_MODEL), lambda bi, i: (bi, i, 0)),
                  _resident((1, D_MODEL)), _resident((D_MODEL, D_IN)),
                  _resident((SEQ_HALO, D_BRANCH)), _resident((1, D_BRANCH)),
                  _resident((1, D_BRANCH)), _resident((1, D_BRANCH)),
                  _resident((D_BRANCH, D_BRANCH)), _resident((1, D_BRANCH))],
        out_specs=[pl.BlockSpec((1, ts, 2 * D_BRANCH), lambda bi, i: (bi, i, 0)),
                   pl.BlockSpec((1, ts, D_ATT), lambda bi, i: (bi, i, 0))],
        out_shape=[jax.ShapeDtypeStruct((b, s, 2 * D_BRANCH), BF16),
                   jax.ShapeDtypeStruct((b, s, D_ATT), BF16)],
        scratch_shapes=[pltpu.VMEM((SEQ_HALO + ts, D_BRANCH), F32),
                        pltpu.VMEM((8, SEQ_HALO + ts, D_BRANCH), F32),
                        pltpu.VMEM((ts, D_BRANCH), F32),
                        pltpu.VMEM((SEQ_HALO + ts, D_BRANCH), F32),
                        pltpu.VMEM((3, SEQ_HALO + ts, D_BRANCH), F32)],
        compiler_params=_params("arbitrary", "arbitrary"),
        name="mix_front",
    )(x3, g, w, conv_w, conv_b, ln_g, ln_b, pool_w_bd, pool_scale)


def _stickbreak_body(q_ref, k_ref, v_ref, o_ref, kexp_ref, vexp_ref, w_ref, p_ref, f_ref, nr_ref,
                     acc_ref):
    tq = q_ref.shape[1]
    tk = SB_KEY_TILE
    n_kb = k_ref.shape[1] // tk
    per_q = tq // tk
    assert per_q % 2 == 0
    qi = pl.program_id(1)
    lane_head = _lane_head()

    @pl.when(qi == 0)
    def _():
        row_head = lax.shift_right_logical(lax.broadcasted_iota(jnp.int32, (D_BRANCH, 1), 0), 6)

        def expand(j, carry):
            start = pl.multiple_of(j * tk, tk)
            kt = k_ref[0, pl.ds(start, tk), :].astype(F32).T
            kexp_ref[j] = jnp.concatenate(
                [jnp.where(row_head == h, kt, 0.0) for h in range(N_HEADS)], axis=1).astype(BF16)
            vexp_ref[j] = _head_expand(v_ref[0, pl.ds(start, tk), :], lane_head)
            return carry

        lax.fori_loop(0, n_kb, expand, 0)

    q = q_ref[0]
    row = lax.broadcasted_iota(jnp.int32, (tk, tk), 0)
    col = lax.broadcasted_iota(jnp.int32, (tk, tk), 1)
    tri = (row >= col).astype(BF16)

    n_blocks = (qi + 1) * per_q

    def key_block(j):
        return jnp.maximum(n_blocks - 1 - j, 0)

    def logits(j, slot, r0=0):
        w_ref[slot, r0:, :] = _dot(q[r0:], kexp_ref[key_block(j)])

    def probs(slot, mask, r0=0):
        neg_run = nr_ref[r0:, :]
        f_ref[slot, r0:, :] = jnp.exp2(neg_run)
        total = None
        for h in range(N_HEADS):
            w = w_ref[slot, r0:, h * tk:(h + 1) * tk]
            sp = jnp.maximum(jnp.log2(1.0 + jnp.exp2(jnp.minimum(w, SB_CLAMP))), w)
            if mask is not None:
                sp = jnp.where(mask, sp, 0.0)
            rev = _dot(sp.astype(BF16), tri)
            p = jnp.exp2(w - rev)
            if mask is not None:
                p = jnp.where(mask, p, 0.0)
            p_ref[slot, r0:, h * tk:(h + 1) * tk] = p.astype(BF16)
            total = rev[:, 0:1] if total is None else jnp.where(lane_head == h, rev[:, 0:1], total)
        nr_ref[r0:, :] = neg_run - total

    def accumulate(j, slot, r0=0):
        acc_ref[r0:, :] += _dot(p_ref[slot, r0:, :], vexp_ref[key_block(j)]) * f_ref[slot, r0:, :]

    acc_ref[...] = jnp.zeros_like(acc_ref)
    nr_ref[...] = jnp.zeros_like(nr_ref)
    t_loc = lax.broadcasted_iota(jnp.int32, (tq, tk), 0)
    s_loc = lax.broadcasted_iota(jnp.int32, (tq, tk), 1)

    def first_row(j):
        return max(per_q - 1 - j, 0) * tk

    logits(0, 0, first_row(0))
    for j in range(per_q):
        logits(j + 1, (j + 1) % 2, first_row(j + 1))
        r0 = first_row(j)
        probs(j % 2, ((r0 + s_loc) < t_loc)[r0:], r0)
        if j > 0:
            accumulate(j - 1, (j - 1) % 2, first_row(j - 1))

    def pair(it, carry):
        j = per_q + 2 * it
        logits(j + 1, 1)
        probs(0, None)
        accumulate(j - 1, 1)
        logits(j + 2, 0)
        probs(1, None)
        accumulate(j, 0)
        return carry

    lax.fori_loop(0, qi * (per_q // 2), pair, 0)
    accumulate(n_blocks - 1, 1)
    o_ref[0] = acc_ref[...].astype(BF16)


def _stickbreak(u3):
    b, s, _ = u3.shape
    tq, tk = SB_QUERY_TILE, SB_KEY_TILE
    return pl.pallas_call(
        _stickbreak_body,
        grid=(b, s // tq),
        in_specs=[pl.BlockSpec((1, tq, D_BRANCH), lambda bi, i: (bi, i, COL_SB_Q)),
                  pl.BlockSpec((1, s, D_BRANCH), lambda bi, i: (bi, 0, COL_SB_K)),
                  pl.BlockSpec((1, s, D_BRANCH), lambda bi, i: (bi, 0, COL_SB_V))],
        out_specs=pl.BlockSpec((1, tq, D_BRANCH), lambda bi, i: (bi, i, 0)),
        out_shape=jax.ShapeDtypeStruct((b, s, D_BRANCH), BF16),
        scratch_shapes=[pltpu.VMEM((s // tk, D_BRANCH, N_HEADS * tk), BF16),
                        pltpu.VMEM((s // tk, N_HEADS * tk, D_BRANCH), BF16),
                        pltpu.VMEM((2, tq, N_HEADS * tk), F32),
                        pltpu.VMEM((2, tq, N_HEADS * tk), BF16),
                        pltpu.VMEM((2, tq, D_BRANCH), F32),
                        pltpu.VMEM((tq, D_BRANCH), F32),
                        pltpu.VMEM((tq, D_BRANCH), F32)],
        compiler_params=_params("arbitrary", "arbitrary"),
        name="stickbreak",
    )(u3, u3, u3)


def _rope_body(pos_ref, f_ref, cos_ref, sin_ref):
    ang = pos_ref[0].astype(F32) * f_ref[...]
    cos_ref[0] = jnp.cos(ang)
    sin_ref[0] = jnp.sin(ang)


def _rope_tables(positions):
    b, s = positions.shape
    half = HEAD_DIM // 2
    inv_freq = ROPE_THETA ** (-jnp.arange(half, dtype=F32) / half)
    freq = jnp.tile(inv_freq, 128 // half)[None, :]
    ts = SEQ_TILE
    out = jax.ShapeDtypeStruct((b, s, 128), F32)
    return pl.pallas_call(
        _rope_body,
        grid=(b, s // ts),
        in_specs=[pl.BlockSpec((1, ts, 1), lambda bi, i: (bi, i, 0)), _resident((1, 128))],
        out_specs=[pl.BlockSpec((1, ts, 128), lambda bi, i: (bi, i, 0))] * 2,
        out_shape=[out, out],
        compiler_params=_params("parallel", "parallel"),
        name="rope_tables",
    )(positions[:, :, None], freq)


def _retention_body(q_ref, k_ref, v_ref, g_ref, cos_ref, sin_ref, dec_ref, xi_ref, zeta_ref,
                    gam_ref, o_ref, state_ref):
    @pl.when(pl.program_id(1) == 0)
    def _():
        state_ref[...] = jnp.zeros_like(state_ref)

    lane = lax.broadcasted_iota(jnp.int32, (1, D_BRANCH), 1)
    lane_head = lax.shift_right_logical(lane, 6)
    first_half = (lane & (HEAD_DIM - 1)) < HEAD_DIM // 2
    r = lax.broadcasted_iota(jnp.int32, (D_BRANCH, D_BRANCH), 0)
    c = lax.broadcasted_iota(jnp.int32, (D_BRANCH, D_BRANCH), 1)
    same_head = lax.shift_right_logical(r, 6) == lax.shift_right_logical(c, 6)
    gm = _group_mean_matrix()

    for n in range(q_ref.shape[0]):
        cos = jnp.concatenate([cos_ref[n], cos_ref[n]], axis=1)
        sin = jnp.concatenate([sin_ref[n], sin_ref[n]], axis=1)

        def rope(x):
            partner = jnp.where(first_half, -pltpu.roll(x, D_BRANCH - HEAD_DIM // 2, 1),
                                pltpu.roll(x, HEAD_DIM // 2, 1))
            return x * cos + partner * sin

        q = rope(q_ref[n].astype(F32))
        k = rope(k_ref[n].astype(F32)) * HEAD_DIM ** -0.5
        v = v_ref[n]
        kexp = _head_expand(k.astype(BF16), lane_head)
        vexp = _head_expand(v, lane_head)
        scores = _dot_nt(q.astype(BF16), kexp) * dec_ref[...]
        inner = _dot(scores.astype(BF16), vexp)
        state = state_ref[n]
        cross = _dot((q * xi_ref[...]).astype(BF16), state.astype(BF16))
        update = _dot_tn((k * zeta_ref[...]).astype(BF16), v)
        state_ref[n] = state * gam_ref[...] + jnp.where(same_head, update, 0.0)

        o = inner + cross
        oc = o - _split_dot(o, gm)
        var = _split_dot(oc * oc, gm)
        o_ref[n] = (_silu(g_ref[n].astype(F32)) * (oc * lax.rsqrt(var + EPS))).astype(BF16)


def _retention(u3, cos_t, sin_t):
    b, s, _ = u3.shape
    tc = SEQ_TILE
    gammas = 1.0 - jnp.exp2(-5.0 - jnp.arange(N_HEADS, dtype=F32))
    log_g = jnp.log(gammas)
    log_g_lane = jnp.repeat(log_g, HEAD_DIM)[None, :]
    idx = jnp.arange(tc, dtype=F32)
    rel = idx[:, None] - idx[None, :]
    decay = jnp.where(rel >= 0, jnp.exp(jnp.maximum(rel, 0.0) * log_g[:, None, None]), 0.0)
    decay = jnp.transpose(decay, (1, 0, 2)).reshape(tc, N_HEADS * tc)
    xi = jnp.exp((idx + 1.0)[:, None] * log_g_lane)
    zeta = jnp.exp((tc - 1.0 - idx)[:, None] * log_g_lane)
    gam = jnp.broadcast_to(jnp.exp(tc * log_g_lane).T, (D_BRANCH, D_BRANCH))

    nb = math.gcd(b, RET_BATCH)

    def col(cb):
        return pl.BlockSpec((nb, tc, D_BRANCH), lambda bi, i: (bi, i, cb))

    tab = pl.BlockSpec((nb, tc, 128), lambda bi, i: (bi, i, 0))
    return pl.pallas_call(
        _retention_body,
        grid=(b // nb, s // tc),
        in_specs=[col(COL_RET_Q), col(COL_RET_K), col(COL_RET_V), col(COL_RET_G), tab, tab,
                  _resident((tc, N_HEADS * tc)), _resident((tc, D_BRANCH)),
                  _resident((tc, D_BRANCH)), _resident((D_BRANCH, D_BRANCH))],
        out_specs=pl.BlockSpec((nb, tc, D_BRANCH), lambda bi, i: (bi, i, 0)),
        out_shape=jax.ShapeDtypeStruct((b, s, D_BRANCH), BF16),
        scratch_shapes=[pltpu.VMEM((nb, D_BRANCH, D_BRANCH), F32)],
        compiler_params=_params("parallel", "arbitrary"),
        name="retention",
    )(u3, u3, u3, u3, cos_t, sin_t, decay, xi, zeta, gam)


def _merge_body(x_ref, ycp_ref, ysb_ref, yret_ref, g_ref, wg_ref, wb_ref, wo_ref, o_ref):
    x = x_ref[...]
    h = _rms_rows(x, g_ref[...]).astype(BF16)
    ycp = ycp_ref[...]
    branches = (ycp[:, :D_BRANCH], ycp[:, D_BRANCH:], ysb_ref[...], yret_ref[...])
    parts = []
    for n in range(D_MODEL // D_BRANCH):
        cols = slice(n * D_BRANCH, (n + 1) * D_BRANCH)
        m = None
        for i, y in enumerate(branches):
            term = _sigmoid(_dot(h, wg_ref[i, :, cols])) * _dot(y, wb_ref[i, :, cols])
            m = term if m is None else m + term
        parts.append(m.astype(BF16))
    o_ref[...] = x + _dot(jnp.concatenate(parts, axis=1), wo_ref[...])


def _merge(x2, ycp, ysb, yret, g, wg, wb, wo):
    t = x2.shape[0]
    tm = TOKEN_TILE

    def rows(width):
        return pl.BlockSpec((tm, width), lambda i: (i, 0))

    return pl.pallas_call(
        _merge_body,
        grid=(t // tm,),
        in_specs=[rows(D_MODEL), rows(2 * D_BRANCH), rows(D_BRANCH), rows(D_BRANCH),
                  _resident((1, D_MODEL)), _resident((N_HEADS, D_MODEL, D_MODEL)),
                  _resident((N_HEADS, D_BRANCH, D_MODEL)), _resident((D_MODEL, D_MODEL))],
        out_specs=rows(D_MODEL),
        out_shape=jax.ShapeDtypeStruct((t, D_MODEL), F32),
        compiler_params=_params("parallel"),
        name="merge",
    )(x2, ycp, ysb, yret, g, wg, wb, wo)


def _head_rms(x, gain_lanes, gm):
    ms = _split_dot(x * x, gm)
    return x * lax.rsqrt(ms + EPS) * gain_lanes


def _mem_kv_body(m_ref, g_ref, w_ref, gk_ref, k_ref, v_ref):
    hm = _rms_rows(m_ref[0], g_ref[...]).astype(BF16)
    kv = _dot(hm, w_ref[...])
    k_ref[0] = _head_rms(kv[:, :D_BRANCH], gk_ref[...], _group_mean_matrix()).astype(BF16)
    v_ref[0] = kv[:, D_BRANCH:].astype(BF16)


def _mem_kv(mem, g, wkv, gk_lanes):
    b, m, _ = mem.shape
    out = jax.ShapeDtypeStruct((b, m, D_BRANCH), BF16)
    blk = pl.BlockSpec((1, m, D_BRANCH), lambda bi: (bi, 0, 0))
    return pl.pallas_call(
        _mem_kv_body,
        grid=(b,),
        in_specs=[pl.BlockSpec((1, m, D_MODEL), lambda bi: (bi, 0, 0)), _resident((1, D_MODEL)),
                  _resident((D_MODEL, 2 * D_BRANCH)), _resident((1, D_BRANCH))],
        out_specs=[blk, blk],
        out_shape=[out, out],
        compiler_params=_params("parallel"),
        name="mem_kv",
    )(mem, g, wkv, gk_lanes)


def _xattn_body(x_ref, k_ref, v_ref, g_ref, wq_ref, gq_ref, wo_ref, o_ref):
    m = k_ref.shape[1]
    gm = _group_mean_matrix()
    lane_head = _lane_head()
    for n in range(x_ref.shape[0]):
        x = x_ref[n]
        h = _rms_rows(x, g_ref[...]).astype(BF16)
        q = _head_rms(_dot(h, wq_ref[...]), gq_ref[...], gm).astype(BF16)
        s_all = _dot_nt(q, _head_expand(k_ref[n], lane_head)) * HEAD_DIM ** -0.5
        probs = []
        for hd in range(N_HEADS):
            s = s_all[:, hd * m:(hd + 1) * m]
            e = jnp.exp(s - jnp.max(s, axis=-1, keepdims=True))
            probs.append((e / jnp.sum(e, axis=-1, keepdims=True)).astype(BF16))
        o = _dot(jnp.concatenate(probs, axis=1), _head_expand(v_ref[n], lane_head))
        o_ref[n] = x + _dot(o.astype(BF16), wo_ref[...])


def _xattn(x3, k, v, g, wq, gq_lanes, wo):
    b, s, _ = x3.shape
    m = k.shape[1]
    ts = TOKEN_TILE
    nb = math.gcd(b, XATTN_BATCH)
    xblk = pl.BlockSpec((nb, ts, D_MODEL), lambda bi, i: (bi, i, 0))
    kvblk = pl.BlockSpec((nb, m, D_BRANCH), lambda bi, i: (bi, 0, 0))
    return pl.pallas_call(
        _xattn_body,
        grid=(b // nb, s // ts),
        in_specs=[xblk, kvblk, kvblk, _resident((1, D_MODEL)), _resident((D_MODEL, D_BRANCH)),
                  _resident((1, D_BRANCH)), _resident((D_BRANCH, D_MODEL))],
        out_specs=xblk,
        out_shape=jax.ShapeDtypeStruct((b, s, D_MODEL), F32),
        compiler_params=_params("parallel", "parallel"),
        name="xattn",
    )(x3, k, v, g, wq, gq_lanes, wo)


def _swiglu_rows(h, wgu_ref, wd_ref, acc_ref, between=None):
    for j in range(D_FF // FF_CHUNK):
        cols = slice(j * FF_CHUNK, (j + 1) * FF_CHUNK)
        up_cols = slice(D_FF + j * FF_CHUNK, D_FF + (j + 1) * FF_CHUNK)
        a = (_silu(_dot(h, wgu_ref[:, cols])) * _dot(h, wgu_ref[:, up_cols])).astype(BF16)
        part = _dot(a, wd_ref[cols, :])
        if j == 0:
            acc_ref[...] = part
        else:
            acc_ref[...] += part
        if between is not None:
            between(j)


def _ffn_body(x_ref, g_ref, wgu_ref, wd_ref, o_ref, acc_ref):
    x = x_ref[...]
    h = _rms_rows(x, g_ref[...]).astype(BF16)
    _swiglu_rows(h, wgu_ref, wd_ref, acc_ref)
    o_ref[...] = x + acc_ref[...]


def _ffn(x2, g, wgu, wd):
    t = x2.shape[0]
    tm = TOKEN_TILE
    rows = pl.BlockSpec((tm, D_MODEL), lambda i: (i, 0))
    return pl.pallas_call(
        _ffn_body,
        grid=(t // tm,),
        in_specs=[rows, _resident((1, D_MODEL)), _resident((D_MODEL, 2 * D_FF)),
                  _resident((D_FF, D_MODEL))],
        out_specs=rows,
        out_shape=jax.ShapeDtypeStruct((t, D_MODEL), F32),
        scratch_shapes=[pltpu.VMEM((tm, D_MODEL), F32)],
        compiler_params=_params("parallel"),
        name="ffn",
    )(x2, g, wgu, wd)


LANE_TILES = D_MODEL // 128


def _to_token_tiles(x, o_ref):
    n = x.shape[0]
    for c in range(LANE_TILES):
        o_ref[pl.ds(c, n, stride=LANE_TILES), :] = x[:, c * 128:(c + 1) * 128]


def _from_token_tiles(ref, n):
    return jnp.concatenate([ref[pl.ds(c, n, stride=LANE_TILES), :] for c in range(LANE_TILES)], axis=1)


def _router_body(x_ref, g_ref, rhi_ref, rlo_ref, o_ref, xt_ref):
    x = x_ref[...]
    _to_token_tiles(x, xt_ref)
    h = _rms_rows(x, g_ref[...])
    hi = h.astype(BF16)
    lo = (h - hi.astype(F32)).astype(BF16)
    logits = _dot(hi, rhi_ref[...]) + (_dot(hi, rlo_ref[...]) + _dot(lo, rhi_ref[...]))
    lane = lax.broadcasted_iota(jnp.int32, logits.shape, 1)
    neg = jnp.float32(-jnp.inf)
    logits = jnp.where(lane < N_EXPERTS, logits, neg)
    m1 = jnp.max(logits, axis=-1, keepdims=True)
    i1 = jnp.min(jnp.where(logits == m1, lane, ROUTER_LANES), axis=-1, keepdims=True)
    rest = jnp.where(lane == i1, neg, logits)
    m2 = jnp.max(rest, axis=-1, keepdims=True)
    i2 = jnp.min(jnp.where(rest == m2, lane, ROUTER_LANES), axis=-1, keepdims=True)
    e2 = jnp.exp(m2 - m1)
    w1 = 1.0 / (1.0 + e2)
    w2 = e2 / (1.0 + e2)
    out = jnp.where(lane == 0, i1.astype(F32), 0.0)
    out = jnp.where(lane == 1, i2.astype(F32), out)
    out = jnp.where(lane == 2, w1, out)
    o_ref[...] = jnp.where(lane == 3, w2, out)


def _router(x2, g, r_hi, r_lo):
    t = x2.shape[0]
    tm = TOKEN_TILE
    return pl.pallas_call(
        _router_body,
        grid=(t // tm,),
        in_specs=[pl.BlockSpec((tm, D_MODEL), lambda i: (i, 0)), _resident((1, D_MODEL)),
                  _resident((D_MODEL, ROUTER_LANES)), _resident((D_MODEL, ROUTER_LANES))],
        out_specs=[pl.BlockSpec((tm, ROUTER_LANES), lambda i: (i, 0)),
                   pl.BlockSpec((tm * LANE_TILES, 128), lambda i: (i, 0))],
        out_shape=[jax.ShapeDtypeStruct((t, ROUTER_LANES), F32),
                   jax.ShapeDtypeStruct((t * LANE_TILES, 128), F32)],
        compiler_params=_params("parallel"),
        name="router",
    )(x2, g, r_hi, r_lo)


def _experts_body(tile_expert_ref, n_tiles_ref, src_ref, src_next_ref, dst_prev_ref, dst_ref,
                  x_hbm, g_ref, wgu_ref, wd_ref, y_hbm, xbuf_ref, ybuf_ref, acc_ref, gsem, ssem):
    i = pl.program_id(0)
    last = pl.num_programs(0) - 1
    slot = lax.rem(i, 2)
    other = 1 - slot
    tm = EXPERT_TILE
    rows = LANE_TILES

    def gather(idx_ref, r, s):
        src = pl.multiple_of(idx_ref[0, 0, r] * rows, rows)
        return pltpu.make_async_copy(x_hbm.at[pl.ds(src, rows)],
                                     xbuf_ref.at[s, pl.ds(r * rows, rows)], gsem)

    def scatter(idx_ref, r, s):
        dst = pl.multiple_of(idx_ref[0, 0, r] * rows, rows)
        return pltpu.make_async_copy(ybuf_ref.at[s, pl.ds(r * rows, rows)],
                                     y_hbm.at[pl.ds(dst, rows)], ssem)

    def wait_gather():
        pltpu.make_async_copy(x_hbm.at[pl.ds(0, tm * rows)], xbuf_ref.at[0], gsem).wait()

    def wait_scatter():
        pltpu.make_async_copy(ybuf_ref.at[0], y_hbm.at[pl.ds(0, tm * rows)], ssem).wait()

    def start_next(r, lane=0):
        gather(src_next_ref, r, other).start(priority=lane)
        scatter(dst_prev_ref, r, other).start(priority=1 - lane)

    @pl.when(i == 0)
    def _():
        ybuf_ref[...] = jnp.zeros_like(ybuf_ref)
        lax.fori_loop(0, tm, lambda r, c: (gather(src_ref, r, 0).start(), c)[1], 0)

    wait_gather()

    @pl.when(i > 0)
    def _():
        wait_scatter()

    @pl.when(i < n_tiles_ref[0])
    def _():
        h = _rms_rows(_from_token_tiles(xbuf_ref.at[slot], tm), g_ref[...]).astype(BF16)
        per = -(-tm // EXPERT_ISSUE_CHUNKS)

        def between(j):
            for r in range(j * per, min((j + 1) * per, tm)):
                start_next(r, r % 2)

        _swiglu_rows(h, wgu_ref.at[0], wd_ref.at[0], acc_ref, between)
        _to_token_tiles(acc_ref[...], ybuf_ref.at[slot])

    @pl.when(i >= n_tiles_ref[0])
    def _():
        ybuf_ref[slot] = jnp.zeros((tm * rows, 128), F32)
        lax.fori_loop(0, tm, lambda r, c: (start_next(r), c)[1], 0)

    @pl.when(i == last)
    def _():
        wait_gather()
        wait_scatter()
        lax.fori_loop(0, tm, lambda r, c: (scatter(dst_ref, r, slot).start(), c)[1], 0)
        wait_scatter()


def _experts(x_tiles, src_tiles, dst_tiles, tile_expert, n_tiles, g, wgu, wd):
    n_steps = src_tiles.shape[0] - 1
    tm = EXPERT_TILE

    def idx(off):
        return pl.BlockSpec((1, 1, tm), lambda i, te, nt: (i + off, 0, 0), memory_space=pltpu.SMEM)

    def expert(shape):
        return pl.BlockSpec((1,) + shape, lambda i, te, nt: (te[i], 0, 0))

    grid_spec = pltpu.PrefetchScalarGridSpec(
        num_scalar_prefetch=2,
        grid=(n_steps,),
        in_specs=[idx(0), idx(1), idx(0), idx(1),
                  pl.BlockSpec(memory_space=pl.ANY),
                  pl.BlockSpec((1, D_MODEL), lambda i, te, nt: (0, 0)),
                  expert((D_MODEL, 2 * D_FF)), expert((D_FF, D_MODEL))],
        out_specs=pl.BlockSpec(memory_space=pl.ANY),
        scratch_shapes=[pltpu.VMEM((2, tm * LANE_TILES, 128), F32),
                        pltpu.VMEM((2, tm * LANE_TILES, 128), F32),
                        pltpu.VMEM((tm, D_MODEL), F32),
                        pltpu.SemaphoreType.DMA(()), pltpu.SemaphoreType.DMA(())],
    )
    return pl.pallas_call(
        _experts_body,
        grid_spec=grid_spec,
        out_shape=jax.ShapeDtypeStruct(((n_steps + 1) * tm * LANE_TILES, 128), F32),
        compiler_params=_params("arbitrary"),
        name="experts",
    )(tile_expert, n_tiles, src_tiles, src_tiles, dst_tiles, dst_tiles, x_tiles, g, wgu, wd)


def _combine_body(x_ref, y0_ref, y1_ref, r_ref, o_ref):
    routed = r_ref[...]
    tm = x_ref.shape[0]
    y0 = _from_token_tiles(y0_ref, tm)
    y1 = _from_token_tiles(y1_ref, tm)
    o_ref[...] = x_ref[...] + (routed[:, 2:3] * y0 + routed[:, 3:4] * y1)


def _combine(x2, y_tiles, routed):
    t = x2.shape[0]
    tm = TOKEN_TILE
    rows = pl.BlockSpec((tm, D_MODEL), lambda i: (i, 0))
    return pl.pallas_call(
        _combine_body,
        grid=(t // tm,),
        in_specs=[rows, pl.BlockSpec((tm * LANE_TILES, 128), lambda i: (i, 0)),
                  pl.BlockSpec((tm * LANE_TILES, 128), lambda i: (i + t // tm, 0)),
                  pl.BlockSpec((tm, ROUTER_LANES), lambda i: (i, 0))],
        out_specs=rows,
        out_shape=jax.ShapeDtypeStruct((t, D_MODEL), F32),
        compiler_params=_params("parallel"),
        name="combine",
    )(x2, y_tiles, y_tiles, routed)


def _moe(x2, g, router, wgu, wd):
    t = x2.shape[0]
    tm = EXPERT_TILE
    r_pad = jnp.pad(router, ((0, 0), (0, ROUTER_LANES - N_EXPERTS)))
    r_hi = r_pad.astype(BF16)
    r_lo = (r_pad - r_hi.astype(F32)).astype(BF16)
    routed, x_tiles = _router(x2, g, r_hi, r_lo)
    expert_of = routed[:, 0:2].astype(jnp.int32).T.reshape(-1)

    n_steps = (2 * t) // tm + N_EXPERTS
    n_pad = n_steps * tm - 2 * t
    counts = jnp.sum((expert_of[:, None] == jnp.arange(N_EXPERTS, dtype=jnp.int32)[None, :])
                     .astype(jnp.int32), axis=0)
    padded = ((counts + tm - 1) // tm) * tm
    pad_ends = jnp.cumsum(padded - counts)
    filler = jnp.arange(n_pad, dtype=jnp.int32)
    filler_expert = jnp.sum((filler[:, None] >= pad_ends[None, :]).astype(jnp.int32), axis=1)
    shift = 1 + (2 * t - 1).bit_length()
    keys = jnp.concatenate([expert_of * (1 << shift) + jnp.arange(2 * t, dtype=jnp.int32),
                            filler_expert * (1 << shift) + (1 << (shift - 1)) + filler])
    vals = jnp.concatenate([jnp.arange(2 * t, dtype=jnp.int32), 2 * t + tm + filler])
    _, dst = lax.sort((keys, vals), num_keys=1)
    src = jnp.where(dst < 2 * t, dst % t, 0)
    spare_tile = 2 * t + jnp.arange(tm, dtype=jnp.int32)
    dst_tiles = jnp.concatenate([spare_tile, dst]).reshape(n_steps + 1, 1, tm)
    src_tiles = jnp.concatenate([src, jnp.zeros((tm,), jnp.int32)]).reshape(n_steps + 1, 1, tm)
    ends = jnp.cumsum(padded)
    n_tiles = (ends[-1] // tm).astype(jnp.int32).reshape(1)
    tile_start = jnp.minimum(jnp.arange(n_steps, dtype=jnp.int32), n_tiles[0] - 1) * tm
    tile_expert = jnp.sum((tile_start[:, None] >= ends[None, :]).astype(jnp.int32), axis=1)

    y_tiles = _experts(x_tiles, src_tiles, dst_tiles, tile_expert, n_tiles, g, wgu, wd)
    return _combine(x2, y_tiles, routed)


def _block_diag(w):
    g, n, _ = w.shape
    eye = jnp.eye(g, dtype=w.dtype)
    return (eye[:, None, :, None] * w[:, :, None, :]).reshape(g * n, g * n)


def kernel(x, mem, positions, norm_mix_g, w_in, conv_w, conv_b, conv_ln_g, conv_ln_b, pool_w, pool_scale, w_gate, w_branch, w_out, norm_xa_g, norm_mem_g, xa_wq, xa_wkv, xa_q_norm_g, xa_k_norm_g, xa_wo, norm_ffn_g, ffn_w_gu, ffn_w_down, moe_router, moe_w_gu, moe_w_down):
    b, s, d = x.shape
    depth = w_in.shape[0]
    t = b * s
    assert d == D_MODEL and s % TOKEN_TILE == 0 and s % SEQ_TILE == 0

    def row(v):
        return v.astype(F32)[None, :]

    cos_t, sin_t = _rope_tables(positions)
    x2 = x.astype(F32).reshape(t, d)
    for l in range(depth):
        cw = jnp.pad(conv_w[l].astype(F32), ((0, SEQ_HALO - CONV_WIDTH), (0, 0)))
        ycp, u_att = _mix_front(x2.reshape(b, s, d), row(norm_mix_g[l]), w_in[l].astype(BF16), cw,
                                row(conv_b[l]), row(conv_ln_g[l]), row(conv_ln_b[l]),
                                _block_diag(pool_w[l]).astype(BF16), row(pool_scale[l]))
        ysb = _stickbreak(u_att)
        yret = _retention(u_att, cos_t, sin_t)
        x2 = _merge(x2, ycp.reshape(t, 2 * D_BRANCH), ysb.reshape(t, D_BRANCH),
                    yret.reshape(t, D_BRANCH), row(norm_mix_g[l]), w_gate[l].astype(BF16),
                    w_branch[l].astype(BF16), w_out[l].astype(BF16))
        k_mem, v_mem = _mem_kv(mem.astype(F32), row(norm_mem_g[l]), xa_wkv[l].astype(BF16),
                               row(jnp.tile(xa_k_norm_g[l], N_HEADS)))
        x2 = _xattn(x2.reshape(b, s, d), k_mem, v_mem, row(norm_xa_g[l]), xa_wq[l].astype(BF16),
                    row(jnp.tile(xa_q_norm_g[l], N_HEADS)), xa_wo[l].astype(BF16)).reshape(t, d)
        g_ffn = row(norm_ffn_g[l])
        if l % 2 == 0:
            x2 = _ffn(x2, g_ffn, ffn_w_gu[l // 2].astype(BF16), ffn_w_down[l // 2].astype(BF16))
        else:
            x2 = _moe(x2, g_ffn, moe_router[l // 2].astype(F32), moe_w_gu[l // 2].astype(BF16),
                      moe_w_down[l // 2].astype(BF16))
    return x2.reshape(b, s, d).astype(x.dtype)
```
